```python
import math
import jax, jax.numpy as jnp
from jax import lax
import numpy as np

D_MODEL = 2048
BATCH = 2
SEQ = 8192
DEPTH = 4

N_MIXERS = 4
BLOCK = 128
EPS = 1e-6
ROPE_THETA = 10000.0
CONV_WIDTH = 3
SGU_WIDTH = D_MODEL
SGU_GROUPS = 16
SGU_CHUNK = 128
DIFF_HEADS = 16
DIFF_HEAD_DIM = D_MODEL // DIFF_HEADS // 2
SB_HEADS = 16
SB_HEAD_DIM = D_MODEL // SB_HEADS
D_FF = 5632
FFN_CONV_WIDTH = 3
N_A = (DEPTH + 3) // 4
N_B = (DEPTH + 2) // 4
N_C = (DEPTH + 1) // 4
N_D = DEPTH // 4

kernel_name = 'interleaved_hybrid_conv_sgu_diffattn_stickbreak'


def rms_norm(x, g):
    xf = x.astype(jnp.float32)
    y = xf * lax.rsqrt(jnp.mean(xf * xf, axis=-1, keepdims=True) + EPS)
    return (y * g.astype(jnp.float32)).astype(x.dtype)


def layer_norm(x, g, b):
    xf = x.astype(jnp.float32)
    mu = jnp.mean(xf, axis=-1, keepdims=True)
    xc = xf - mu
    y = xc * lax.rsqrt(jnp.mean(xc * xc, axis=-1, keepdims=True) + EPS)
    return (y * g.astype(jnp.float32) + b.astype(jnp.float32)).astype(x.dtype)


def causal_dwconv(x, w):
    width, ch = w.shape
    return lax.conv_general_dilated(
        x, w[:, None, :].astype(x.dtype), window_strides=(1,), padding=[(width - 1, 0)],
        dimension_numbers=('NWC', 'WIO', 'NWC'), feature_group_count=ch)


def rope_tables(positions, dim):
    inv_freq = 1.0 / (ROPE_THETA ** (jnp.arange(0, dim, 2, dtype=jnp.float32) / dim))
    ang = positions.astype(jnp.float32)[..., None] * inv_freq
    return jnp.cos(ang)[:, :, None, :], jnp.sin(ang)[:, :, None, :]


def apply_rope(x, cos, sin):
    x1, x2 = jnp.split(x.astype(jnp.float32), 2, axis=-1)
    return jnp.concatenate([x1 * cos - x2 * sin, x2 * cos + x1 * sin], axis=-1).astype(x.dtype)


def to_query_blocks(t):
    b, s = t.shape[:2]
    return jnp.moveaxis(t.reshape(b, s // BLOCK, BLOCK, *t.shape[2:]), 1, 0)


def from_query_blocks(t):
    t = jnp.moveaxis(t, 0, 1)
    return t.reshape(t.shape[0], t.shape[1] * t.shape[2], *t.shape[3:])


def short_conv_mixer(h, w_in, conv_w, w_out):
    gate_b, gate_c, xin = jnp.split(h @ w_in, 3, axis=-1)
    return (gate_b * causal_dwconv(gate_c * xin, conv_w)) @ w_out


def spatial_gating_mixer(h, w_in, ln_g, ln_b, w_s, b_s, w_out):
    b, s, _ = h.shape
    u, v = jnp.split(jax.nn.gelu(h @ w_in), 2, axis=-1)
    v = layer_norm(v, ln_g, ln_b)
    v = v.reshape(b, s // SGU_CHUNK, SGU_CHUNK, SGU_GROUPS, SGU_WIDTH // SGU_GROUPS)
    causal = jnp.tril(jnp.ones((SGU_CHUNK, SGU_CHUNK), dtype=bool))
    w_causal = jnp.where(causal[None], w_s, 0.0)
    mixed = jnp.einsum('gts,bcsgd->bctgd', w_causal, v) + b_s.T[:, :, None]
    return (u * mixed.reshape(b, s, SGU_WIDTH)) @ w_out


def differential_attention(h, w_qkv, lq1, lk1, lq2, lk2, subln_g, w_out, cos, sin, lambda_init):
    b, s, _ = h.shape
    H, d = DIFF_HEADS, DIFF_HEAD_DIM
    q, k, v = jnp.split(h @ w_qkv, 3, axis=-1)
    q = apply_rope(q.reshape(b, s, 2 * H, d), cos, sin).reshape(b, s, H, 2, d)
    k = apply_rope(k.reshape(b, s, 2 * H, d), cos, sin).reshape(b, s, H, 2, d)
    v = v.reshape(b, s, H, 2 * d)
    lam = (jnp.exp(jnp.sum(lq1.astype(jnp.float32) * lk1.astype(jnp.float32)))
           - jnp.exp(jnp.sum(lq2.astype(jnp.float32) * lk2.astype(jnp.float32))) + lambda_init)
    scale = d ** -0.5
    k_pos = jnp.arange(s)

    def block(args):
        qb, i = args
        scores = jnp.einsum('bqhcd,bkhcd->bhcqk', qb, k).astype(jnp.float32) * scale
        q_pos = i * BLOCK + jnp.arange(BLOCK)
        mask = k_pos[None, :] <= q_pos[:, None]
        p = jax.nn.softmax(jnp.where(mask, scores, -jnp.inf), axis=-1)
        attn = p[:, :, 0] - lam * p[:, :, 1]
        return jnp.einsum('bhqk,bkhe->bqhe', attn.astype(v.dtype), v)

    o = from_query_blocks(lax.map(block, (to_query_blocks(q), jnp.arange(s // BLOCK))))
    o = rms_norm(o, subln_g) * (1.0 - lambda_init)
    return o.reshape(b, s, H * 2 * d) @ w_out


def stick_breaking_attention(h, w_qkv, w_out):
    b, s, _ = h.shape
    H, d = SB_HEADS, SB_HEAD_DIM
    q, k, v = [t.reshape(b, s, H, d) for t in jnp.split(h @ w_qkv, 3, axis=-1)]
    scale = d ** -0.5
    k_pos = jnp.arange(s)

    def block(args):
        qb, i = args
        z = jnp.einsum('bqhd,bkhd->bhqk', qb, k).astype(jnp.float32) * scale
        q_pos = i * BLOCK + jnp.arange(BLOCK)
        mask = k_pos[None, :] < q_pos[:, None]
        log_beta = jax.nn.log_sigmoid(z)
        log_1m = jnp.where(mask, log_beta - z, 0.0)
        log_survive = lax.cumsum(log_1m, axis=log_1m.ndim - 1, reverse=True) - log_1m
        a = jnp.where(mask, jnp.exp(log_beta + log_survive), 0.0)
        return jnp.einsum('bhqk,bkhd->bqhd', a.astype(v.dtype), v)

    o = from_query_blocks(lax.map(block, (to_query_blocks(q), jnp.arange(s // BLOCK))))
    return o.reshape(b, s, H * d) @ w_out


def conv_ffn(h, w_gate, w_up, conv_w, conv_b, w_down):
    g = causal_dwconv(h @ w_gate, conv_w) + conv_b
    return (jax.nn.silu(g) * (h @ w_up)) @ w_down


def setup_inputs(seed: int = 0) -> dict:
    key = jax.random.key(seed)
    ks = iter(jax.random.split(key, 32))
    f32 = jnp.float32

    def w(shape, fan_in):
        return jax.random.normal(next(ks), shape, f32) * fan_in ** -0.5

    def gain(shape):
        return 1.0 + 0.02 * jax.random.normal(next(ks), shape, f32)

    def small(shape, s=0.02):
        return s * jax.random.normal(next(ks), shape, f32)

    D = D_MODEL
    dH, dd = DIFF_HEADS, DIFF_HEAD_DIM
    return {
        'x': jax.random.normal(next(ks), (BATCH, SEQ, D), f32),
        'positions': jnp.tile(jnp.arange(SEQ, dtype=jnp.int32)[None, :], (BATCH, 1)),
        'norm_mix_g': gain((DEPTH, D)),
        'norm_ffn_g': gain((DEPTH, D)),
        'norm_final_g': gain((D,)),
        'sc_w_in': w((N_A, D, 3 * D), D),
        'sc_conv_w': w((N_A, CONV_WIDTH, D), CONV_WIDTH),
        'sc_w_out': w((N_A, D, D), D),
        'sg_w_in': w((N_B, D, 2 * SGU_WIDTH), D),
        'sg_ln_g': gain((N_B, SGU_WIDTH)),
        'sg_ln_b': small((N_B, SGU_WIDTH)),
        'sg_w_s': w((N_B, SGU_GROUPS, SGU_CHUNK, SGU_CHUNK), SGU_CHUNK),
        'sg_b_s': gain((N_B, SGU_GROUPS, SGU_CHUNK)),
        'sg_w_out': w((N_B, SGU_WIDTH, D), SGU_WIDTH),
        'da_w_qkv': w((N_C, D, 6 * dH * dd), D),
        'da_lambda_q1': small((N_C, dd), 0.1),
        'da_lambda_k1': small((N_C, dd), 0.1),
        'da_lambda_q2': small((N_C, dd), 0.1),
        'da_lambda_k2': small((N_C, dd), 0.1),
        'da_subln_g': gain((N_C, 2 * dd)),
        'da_w_out': w((N_C, 2 * dH * dd, D), 2 * dH * dd),
        'sb_w_qkv': w((N_D, D, 3 * SB_HEADS * SB_HEAD_DIM), D),
        'sb_w_out': w((N_D, SB_HEADS * SB_HEAD_DIM, D), SB_HEADS * SB_HEAD_DIM),
        'ffn_w_gate': w((DEPTH, D, D_FF), D),
        'ffn_w_up': w((DEPTH, D, D_FF), D),
        'ffn_conv_w': w((DEPTH, FFN_CONV_WIDTH, D_FF), FFN_CONV_WIDTH),
        'ffn_conv_b': small((DEPTH, D_FF)),
        'ffn_w_down': w((DEPTH, D_FF, D), D_FF),
    }


def reference(x, positions, norm_mix_g, norm_ffn_g, norm_final_g,
              sc_w_in, sc_conv_w, sc_w_out,
              sg_w_in, sg_ln_g, sg_ln_b, sg_w_s, sg_b_s, sg_w_out,
              da_w_qkv, da_lambda_q1, da_lambda_k1, da_lambda_q2, da_lambda_k2, da_subln_g, da_w_out,
              sb_w_qkv, sb_w_out,
              ffn_w_gate, ffn_w_up, ffn_conv_w, ffn_conv_b, ffn_w_down):
    cos, sin = rope_tables(positions, DIFF_HEAD_DIM)
    h = x
    for layer in range(DEPTH):
        mixer, j = layer % N_MIXERS, layer // N_MIXERS
        a = rms_norm(h, norm_mix_g[layer])
        if mixer == 0:
            y = short_conv_mixer(a, sc_w_in[j], sc_conv_w[j], sc_w_out[j])
        elif mixer == 1:
            y = spatial_gating_mixer(a, sg_w_in[j], sg_ln_g[j], sg_ln_b[j], sg_w_s[j], sg_b_s[j], sg_w_out[j])
        elif mixer == 2:
            lambda_init = 0.8 - 0.6 * math.exp(-0.3 * layer)
            y = differential_attention(a, da_w_qkv[j], da_lambda_q1[j], da_lambda_k1[j],
                                       da_lambda_q2[j], da_lambda_k2[j], da_subln_g[j], da_w_out[j],
                                       cos, sin, lambda_init)
        else:
            y = stick_breaking_attention(a, sb_w_qkv[j], sb_w_out[j])
        h = h + y
        h = h + conv_ffn(rms_norm(h, norm_ffn_g[layer]), ffn_w_gate[layer], ffn_w_up[layer],
                         ffn_conv_w[layer], ffn_conv_b[layer], ffn_w_down[layer])
    return rms_norm(h, norm_final_g)
```

```python
import functools
import math

import jax
import jax.numpy as jnp
from jax import lax
from jax.experimental import pallas as pl
from jax.experimental.pallas import tpu as pltpu

F32 = jnp.float32
BF16 = jnp.bfloat16

EPS = 1e-6
ROPE_THETA = 10000.0
CONV_WIDTH = 3
SGU_GROUPS = 16
SGU_CHUNK = 128
DIFF_HEADS = 16
SB_HEADS = 16
HEAD_WIDTH = 128

V7X_LANES = 128
V7X_SUBLANES = 8
V7X_VMEM_BYTES = 64 * 1024 * 1024
INTERNAL_SCRATCH_BYTES = 12 * 1024 * 1024


def _nbytes(shape, dtype):
    return math.prod(shape) * jnp.dtype(dtype).itemsize


def _params(semantics, pipelined_bytes, resident_bytes=0):
    limit = 2 * pipelined_bytes + resident_bytes + INTERNAL_SCRATCH_BYTES
    limit = min(limit, V7X_VMEM_BYTES - 4 * 1024 * 1024)
    return pltpu.CompilerParams(dimension_semantics=semantics, vmem_limit_bytes=int(limit))


def _row_tile(n, want):
    t = min(n, want)
    assert n % t == 0, (n, t)
    return t


def _rms(x, g):
    ms = jnp.mean(x * x, axis=-1, keepdims=True)
    return x * lax.rsqrt(ms + EPS) * g


def _shift_rows(x, prev, k):
    row = lax.broadcasted_iota(jnp.int32, x.shape, 0)
    out = pltpu.roll(x, k, 0)
    for r in range(k):
        out = jnp.where(row == r, prev[V7X_SUBLANES - k + r:V7X_SUBLANES - k + r + 1, :], out)
    return out


def _rmsnorm_kernel(x_ref, g_ref, o_ref):
    o_ref[...] = _rms(x_ref[...], g_ref[...]).astype(o_ref.dtype)


def _rmsnorm(h, g):
    n, d = h.shape
    tm = _row_tile(n, 512)
    return pl.pallas_call(
        _rmsnorm_kernel,
        grid=(n // tm,),
        in_specs=[pl.BlockSpec((tm, d), lambda i: (i, 0)), pl.BlockSpec((1, d), lambda i: (0, 0))],
        out_specs=pl.BlockSpec((tm, d), lambda i: (i, 0)),
        out_shape=jax.ShapeDtypeStruct((n, d), BF16),
        compiler_params=_params(("arbitrary",), _nbytes((tm, d), F32) + _nbytes((tm, d), BF16)),
        name="rmsnorm",
    )(h, g.reshape(1, d))


def _out_proj_kernel(x_ref, w_ref, h_ref, g_ref, *out_refs, emit_h):
    hn = h_ref[...] + jnp.dot(x_ref[...], w_ref[...], preferred_element_type=F32)
    if emit_h:
        out_refs[0][...] = hn
    an_ref = out_refs[-1]
    an_ref[...] = _rms(hn, g_ref[...]).astype(an_ref.dtype)


def _out_proj(x, w, h, g, *, final=False):
    n, k = x.shape
    d = w.shape[1]
    tm = _row_tile(n, 256)
    an_dtype = F32 if final else BF16
    out_shape = [jax.ShapeDtypeStruct((n, d), an_dtype)]
    out_specs = [pl.BlockSpec((tm, d), lambda i: (i, 0))]
    if not final:
        out_shape.insert(0, jax.ShapeDtypeStruct((n, d), F32))
        out_specs.insert(0, pl.BlockSpec((tm, d), lambda i: (i, 0)))
    pipelined = (_nbytes((tm, k), BF16) + 2 * _nbytes((tm, d), F32) + _nbytes((tm, d), an_dtype))
    outs = pl.pallas_call(
        functools.partial(_out_proj_kernel, emit_h=not final),
        grid=(n // tm,),
        in_specs=[
            pl.BlockSpec((tm, k), lambda i: (i, 0)),
            pl.BlockSpec((k, d), lambda i: (0, 0), pipeline_mode=pl.Buffered(1)),
            pl.BlockSpec((tm, d), lambda i: (i, 0)),
            pl.BlockSpec((1, d), lambda i: (0, 0)),
        ],
        out_specs=out_specs,
        out_shape=out_shape,
        compiler_params=_params(("arbitrary",), pipelined, _nbytes((k, d), BF16)),
        name="out_proj",
    )(x, w, h, g.reshape(1, d))
    return (None, outs[0]) if final else (outs[0], outs[1])


def _ffn_up_kernel(a_ref, wg_ref, wu_ref, cw_ref, cb_ref, t_ref, carry_ref, *, tm, seq):
    i = pl.program_id(1)

    @pl.when((i * tm) % seq == 0)
    def _():
        carry_ref[...] = jnp.zeros_like(carry_ref)

    a = a_ref[...]
    g = jnp.dot(a, wg_ref[...], preferred_element_type=F32)
    up = jnp.dot(a, wu_ref[...], preferred_element_type=F32)
    prev = carry_ref[...]
    cw = cw_ref[...]
    conv = (cw[0:1, :] * _shift_rows(g, prev, 2) + cw[1:2, :] * _shift_rows(g, prev, 1)
            + cw[2:3, :] * g + cb_ref[...])
    t_ref[...] = (conv * jax.nn.sigmoid(conv) * up).astype(t_ref.dtype)
    carry_ref[...] = g[tm - V7X_SUBLANES:, :]


def _ffn_up(a, wg, wu, conv_w, conv_b, *, seq):
    n, d = a.shape
    f = wg.shape[1]
    tm = _row_tile(seq, 1024)
    tn = 512
    assert f % tn == 0
    pipelined = (_nbytes((tm, d), BF16) + 2 * _nbytes((d, tn), BF16) + _nbytes((tm, tn), BF16)
                 + 2 * _nbytes((tm, tn), F32))
    return pl.pallas_call(
        functools.partial(_ffn_up_kernel, tm=tm, seq=seq),
        grid=(f // tn, n // tm),
        in_specs=[
            pl.BlockSpec((tm, d), lambda j, i: (i, 0)),
            pl.BlockSpec((d, tn), lambda j, i: (0, j)),
            pl.BlockSpec((d, tn), lambda j, i: (0, j)),
            pl.BlockSpec((CONV_WIDTH, tn), lambda j, i: (0, j)),
            pl.BlockSpec((1, tn), lambda j, i: (0, j)),
        ],
        out_specs=pl.BlockSpec((tm, tn), lambda j, i: (i, j)),
        out_shape=jax.ShapeDtypeStruct((n, f), BF16),
        scratch_shapes=[pltpu.VMEM((V7X_SUBLANES, tn), F32)],
        compiler_params=_params(("arbitrary", "arbitrary"), pipelined),
        name="ffn_up",
    )(a, wg, wu, conv_w, conv_b.reshape(1, f))


def _sc_in_kernel(a_ref, wb_ref, wc_ref, wx_ref, gb_ref, u_ref):
    a = a_ref[...]
    gb_ref[...] = jnp.dot(a, wb_ref[...], preferred_element_type=F32)
    gc = jnp.dot(a, wc_ref[...], preferred_element_type=F32)
    xi = jnp.dot(a, wx_ref[...], preferred_element_type=F32)
    u_ref[...] = gc * xi


def _sc_in(a, w_in):
    n, d = a.shape
    tm = _row_tile(n, 1024)
    tn = 512
    nj = d // tn
    pipelined = _nbytes((tm, d), BF16) + 3 * _nbytes((d, tn), BF16) + 3 * _nbytes((tm, tn), F32)
    return pl.pallas_call(
        _sc_in_kernel,
        grid=(nj, n // tm),
        in_specs=[
            pl.BlockSpec((tm, d), lambda j, i: (i, 0)),
            pl.BlockSpec((d, tn), lambda j, i: (0, j)),
            pl.BlockSpec((d, tn), lambda j, i: (0, j + nj)),
            pl.BlockSpec((d, tn), lambda j, i: (0, j + 2 * nj)),
        ],
        out_specs=[pl.BlockSpec((tm, tn), lambda j, i: (i, j))] * 2,
        out_shape=[jax.ShapeDtypeStruct((n, d), F32)] * 2,
        compiler_params=_params(("arbitrary", "arbitrary"), pipelined),
        name="sc_in",
    )(a, w_in, w_in, w_in)


def _sc_conv_kernel(u_ref, halo_ref, gb_ref, cw_ref, y_ref, *, tm, seq):
    i = pl.program_id(0)
    u = u_ref[...]
    prev = jnp.where((i * tm) % seq == 0, 0.0, halo_ref[...])
    cw = cw_ref[...]
    conv = cw[0:1, :] * _shift_rows(u, prev, 2) + cw[1:2, :] * _shift_rows(u, prev, 1) + cw[2:3, :] * u
    y_ref[...] = (gb_ref[...] * conv).astype(y_ref.dtype)


def _sc_conv(u, gb, conv_w, *, seq):
    n, d = u.shape
    tm = _row_tile(seq, 512)
    hb = tm // V7X_SUBLANES
    pipelined = 2 * _nbytes((tm, d), F32) + _nbytes((tm, d), BF16) + _nbytes((tm, d), F32)
    return pl.pallas_call(
        functools.partial(_sc_conv_kernel, tm=tm, seq=seq),
        grid=(n // tm,),
        in_specs=[
            pl.BlockSpec((tm, d), lambda i: (i, 0)),
            pl.BlockSpec((V7X_SUBLANES, d), lambda i: (jnp.maximum(i * hb - 1, 0), 0)),
            pl.BlockSpec((tm, d), lambda i: (i, 0)),
            pl.BlockSpec((CONV_WIDTH, d), lambda i: (0, 0)),
        ],
        out_specs=pl.BlockSpec((tm, d), lambda i: (i, 0)),
        out_shape=jax.ShapeDtypeStruct((n, d), BF16),
        compiler_params=_params(("arbitrary",), pipelined),
        name="sc_conv",
    )(u, u, gb, conv_w)


def _sg_in_kernel(a_ref, wu_ref, wv_ref, u_ref, v_ref):
    a = a_ref[...]
    u_ref[...] = jax.nn.gelu(jnp.dot(a, wu_ref[...], preferred_element_type=F32))
    v_ref[...] = jax.nn.gelu(jnp.dot(a, wv_ref[...], preferred_element_type=F32))


def _sg_in(a, w_in):
    n, d = a.shape
    width = w_in.shape[1] // 2
    tm = _row_tile(n, 1024)
    tn = 512
    nj = width // tn
    pipelined = _nbytes((tm, d), BF16) + 2 * _nbytes((d, tn), BF16) + 2 * _nbytes((tm, tn), F32)
    return pl.pallas_call(
        _sg_in_kernel,
        grid=(nj, n // tm),
        in_specs=[
            pl.BlockSpec((tm, d), lambda j, i: (i, 0)),
            pl.BlockSpec((d, tn), lambda j, i: (0, j)),
            pl.BlockSpec((d, tn), lambda j, i: (0, j + nj)),
        ],
        out_specs=[pl.BlockSpec((tm, tn), lambda j, i: (i, j))] * 2,
        out_shape=[jax.ShapeDtypeStruct((n, width), F32)] * 2,
        compiler_params=_params(("arbitrary", "arbitrary"), pipelined),
        name="sg_in",
    )(a, w_in, w_in)


def _sg_mix_kernel(u_ref, v_ref, lng_ref, lnb_ref, ws_ref, bs_ref, y_ref, vn_ref, *, tm):
    v = v_ref[...]
    mu = jnp.mean(v, axis=-1, keepdims=True)
    vc = v - mu
    var = jnp.mean(vc * vc, axis=-1, keepdims=True)
    vn_ref[...] = (vc * lax.rsqrt(var + EPS) * lng_ref[...] + lnb_ref[...]).astype(vn_ref.dtype)
    t = SGU_CHUNK
    nchunk = tm // t
    row = lax.broadcasted_iota(jnp.int32, (t, t), 0)
    col = lax.broadcasted_iota(jnp.int32, (t, t), 1)
    bs = bs_ref[...]
    for g in range(SGU_GROUPS):
        cols = slice(g * t, (g + 1) * t)
        w = jnp.where(row >= col, ws_ref[g], 0.0).astype(BF16)
        rhs = jnp.concatenate([vn_ref[c * t:(c + 1) * t, cols] for c in range(nchunk)], axis=1)
        mixed = jnp.dot(w, rhs, preferred_element_type=F32) + bs[:, g:g + 1]
        for c in range(nchunk):
            rows = slice(c * t, (c + 1) * t)
            y_ref[rows, cols] = (u_ref[rows, cols] * mixed[:, c * t:(c + 1) * t]).astype(y_ref.dtype)


def _sg_mix(u, v, ln_g, ln_b, w_s, b_s):
    n, width = u.shape
    tm = _row_tile(n, 512)
    assert tm % SGU_CHUNK == 0
    pipelined = 2 * _nbytes((tm, width), F32) + _nbytes((tm, width), BF16)
    resident = 2 * _nbytes(w_s.shape, F32) + _nbytes((tm, width), BF16)
    return pl.pallas_call(
        functools.partial(_sg_mix_kernel, tm=tm),
        grid=(n // tm,),
        in_specs=[
            pl.BlockSpec((tm, width), lambda i: (i, 0)),
            pl.BlockSpec((tm, width), lambda i: (i, 0)),
            pl.BlockSpec((1, width), lambda i: (0, 0)),
            pl.BlockSpec((1, width), lambda i: (0, 0)),
            pl.BlockSpec(w_s.shape, lambda i: (0, 0, 0)),
            pl.BlockSpec((SGU_CHUNK, SGU_GROUPS), lambda i: (0, 0)),
        ],
        out_specs=pl.BlockSpec((tm, width), lambda i: (i, 0)),
        out_shape=jax.ShapeDtypeStruct((n, width), BF16),
        scratch_shapes=[pltpu.VMEM((tm, width), BF16)],
        compiler_params=_params(("arbitrary",), pipelined, resident),
        name="sg_mix",
    )(u, v, ln_g.reshape(1, width), ln_b.reshape(1, width), w_s, b_s.T)


ATTN_TQ = 256
ATTN_TK = 256


def _rope_table_kernel(pos_ref, invf_ref, sign_ref, cos_ref, sin_ref):
    ang = pos_ref[...] * invf_ref[...]
    cos_ref[...] = jnp.cos(ang)
    sin_ref[...] = jnp.sin(ang) * sign_ref[...]


def _rope_tables(positions, head_dim):
    n = positions.size
    half = head_dim // 2
    inv_freq = 1.0 / (ROPE_THETA ** (jnp.arange(0, head_dim, 2, dtype=F32) / head_dim))
    lane = jnp.arange(V7X_LANES)
    invf = inv_freq[lane % half].reshape(1, V7X_LANES)
    sign = jnp.where(lane % head_dim < half, -1.0, 1.0).astype(F32).reshape(1, V7X_LANES)
    tm = _row_tile(n, 1024)
    return pl.pallas_call(
        _rope_table_kernel,
        grid=(n // tm,),
        in_specs=[
            pl.BlockSpec((tm, 1), lambda i: (i, 0)),
            pl.BlockSpec((1, V7X_LANES), lambda i: (0, 0)),
            pl.BlockSpec((1, V7X_LANES), lambda i: (0, 0)),
        ],
        out_specs=[pl.BlockSpec((tm, V7X_LANES), lambda i: (i, 0))] * 2,
        out_shape=[jax.ShapeDtypeStruct((n, V7X_LANES), F32)] * 2,
        compiler_params=_params(("arbitrary",), 3 * _nbytes((tm, V7X_LANES), F32)),
        name="rope_tables",
    )(positions.astype(F32).reshape(n, 1), invf, sign)


def _rotate_half(x, cos, sin, first_half):
    partner = jnp.where(first_half, pltpu.roll(x, 96, 1), pltpu.roll(x, 32, 1))
    return x * cos + partner * sin


def _qkv_kernel(a_ref, wq_ref, wk_ref, wv_ref, *refs, rope, q_scale, tk):
    if rope:
        cos_ref, sin_ref, q_ref, k_ref, vt_ref = refs
    else:
        q_ref, k_ref, vt_ref = refs
    a = a_ref[...]
    q = jnp.dot(a, wq_ref[...], preferred_element_type=F32)
    k = jnp.dot(a, wk_ref[...], preferred_element_type=F32)
    v = jnp.dot(a, wv_ref[...], preferred_element_type=F32)
    tm, tn = q.shape
    if rope:
        cos = cos_ref[...]
        sin = sin_ref[...]
        lane = lax.broadcasted_iota(jnp.int32, cos.shape, 1)
        first_half = (lane % 64) < 32
        for c in range(tn // V7X_LANES):
            cols = slice(c * V7X_LANES, (c + 1) * V7X_LANES)
            q_ref[:, cols] = (_rotate_half(q[:, cols], cos, sin, first_half) * q_scale).astype(q_ref.dtype)
            k_ref[:, cols] = _rotate_half(k[:, cols], cos, sin, first_half).astype(k_ref.dtype)
    else:
        q_ref[...] = q.astype(q_ref.dtype)
        k_ref[...] = k.astype(k_ref.dtype)
    for hh in range(tn // HEAD_WIDTH):
        for kk in range(tm // tk):
            blk = v[kk * tk:(kk + 1) * tk, hh * HEAD_WIDTH:(hh + 1) * HEAD_WIDTH]
            vt_ref[hh, kk] = blk.T.astype(vt_ref.dtype)


def _qkv(a, w_qkv, *, tk, rope_tables=None, q_scale=1.0):
    n, d = a.shape
    width = w_qkv.shape[1] // 3
    tm = _row_tile(n, 1024)
    tn = 512
    nj = width // tn
    rope = rope_tables is not None
    in_specs = [
        pl.BlockSpec((tm, d), lambda j, i: (i, 0)),
        pl.BlockSpec((d, tn), lambda j, i: (0, j)),
        pl.BlockSpec((d, tn), lambda j, i: (0, j + nj)),
        pl.BlockSpec((d, tn), lambda j, i: (0, j + 2 * nj)),
    ]
    args = [a, w_qkv, w_qkv, w_qkv]
    if rope:
        in_specs += [pl.BlockSpec((tm, V7X_LANES), lambda j, i: (i, 0))] * 2
        args += list(rope_tables)
    heads = width // HEAD_WIDTH
    pipelined = (_nbytes((tm, d), BF16) + 3 * _nbytes((d, tn), BF16) + 3 * _nbytes((tm, tn), BF16)
                 + 3 * _nbytes((tm, tn), F32) + 2 * _nbytes((tm, V7X_LANES), F32))
    return pl.pallas_call(
        functools.partial(_qkv_kernel, rope=rope, q_scale=q_scale, tk=tk),
        grid=(nj, n // tm),
        in_specs=in_specs,
        out_specs=[
            pl.BlockSpec((tm, tn), lambda j, i: (i, j)),
            pl.BlockSpec((tm, tn), lambda j, i: (i, j)),
            pl.BlockSpec((tn // HEAD_WIDTH, tm // tk, HEAD_WIDTH, tk), lambda j, i: (j, i, 0, 0)),
        ],
        out_shape=[
            jax.ShapeDtypeStruct((n, width), BF16),
            jax.ShapeDtypeStruct((n, width), BF16),
            jax.ShapeDtypeStruct((heads, n // tk, HEAD_WIDTH, tk), BF16),
        ],
        compiler_params=_params(("arbitrary", "arbitrary"), pipelined),
        name="qkv_rope" if rope else "qkv",
    )(*args)


def _attn_specs(seq, tq, tk):
    nq = seq // tq
    nk = seq // tk
    q_spec = pl.BlockSpec((tq, HEAD_WIDTH), lambda b, h, qi: (b * nq + qi, h))
    k_spec = pl.BlockSpec((seq, HEAD_WIDTH), lambda b, h, qi: (b, h))
    vt_spec = pl.BlockSpec((None, nk, HEAD_WIDTH, tk), lambda b, h, qi: (h, b, 0, 0))
    o_spec = pl.BlockSpec((tq, HEAD_WIDTH), lambda b, h, qi: (b * nq + qi, h))
    return q_spec, k_spec, vt_spec, o_spec


def _attn_vmem(seq, tq, tk):
    pipelined = 2 * _nbytes((tq, HEAD_WIDTH), BF16) + 2 * _nbytes((seq, HEAD_WIDTH), BF16)
    return _params(("arbitrary", "arbitrary", "arbitrary"), pipelined)


def _diff_attn_kernel(q_ref, k_ref, vt_ref, lq1_ref, lk1_ref, lq2_ref, lk2_ref, g_ref, o_ref, *,
                      tq, tk, lambda_init):
    qi = pl.program_id(2)
    q = q_ref[...]
    lane = lax.broadcasted_iota(jnp.int32, q.shape, 1)
    zero = jnp.zeros_like(q)
    qq = jnp.concatenate([jnp.where(lane < 64, q, zero), jnp.where(lane >= 64, q, zero)], axis=0)

    def block(kb, carry, diagonal):
        m, l, acc = carry
        kblk = k_ref[pl.ds(pl.multiple_of(kb * tk, tk), tk), :]
        s = lax.dot_general(kblk, qq, (((1,), (1,)), ((), ())), preferred_element_type=F32)
        if diagonal:
            key = kb * tk + lax.broadcasted_iota(jnp.int32, s.shape, 0)
            qry = qi * tq + (lax.broadcasted_iota(jnp.int32, s.shape, 1) % tq)
            s = jnp.where(key <= qry, s, -jnp.inf)
        m_new = jnp.maximum(m, jnp.max(s, axis=0, keepdims=True))
        alpha = jnp.exp(m - m_new)
        p = jnp.exp(s - m_new)
        l = alpha * l + jnp.sum(p, axis=0, keepdims=True)
        acc = alpha * acc + jnp.dot(vt_ref[kb], p.astype(BF16), preferred_element_type=F32)
        return m_new, l, acc

    init = (jnp.full((1, 2 * tq), -jnp.inf, F32), jnp.zeros((1, 2 * tq), F32),
            jnp.zeros((HEAD_WIDTH, 2 * tq), F32))
    carry = lax.fori_loop(0, qi, lambda kb, c: block(kb, c, False), init)
    _, l, acc = block(qi, carry, True)

    o = acc / l
    lam = (jnp.exp(jnp.sum(lq1_ref[...] * lk1_ref[...], axis=-1, keepdims=True))
           - jnp.exp(jnp.sum(lq2_ref[...] * lk2_ref[...], axis=-1, keepdims=True)) + lambda_init)
    o = o[:, :tq] - lam * o[:, tq:]
    ms = jnp.mean(o * o, axis=0, keepdims=True)
    o = o * lax.rsqrt(ms + EPS) * g_ref[...] * (1.0 - lambda_init)
    o_ref[...] = o.T.astype(o_ref.dtype)


def _diff_attn(q, k, vt, lq1, lk1, lq2, lk2, subln_g, *, batch, seq, lambda_init):
    n, width = q.shape
    tq, tk = ATTN_TQ, ATTN_TK
    q_spec, k_spec, vt_spec, o_spec = _attn_specs(seq, tq, tk)
    dd = lq1.shape[0]
    small = pl.BlockSpec((1, dd), lambda b, h, qi: (0, 0))
    return pl.pallas_call(
        functools.partial(_diff_attn_kernel, tq=tq, tk=tk, lambda_init=lambda_init),
        grid=(batch, width // HEAD_WIDTH, seq // tq),
        in_specs=[q_spec, k_spec, vt_spec, small, small, small, small,
                  pl.BlockSpec((HEAD_WIDTH, 1), lambda b, h, qi: (0, 0))],
        out_specs=o_spec,
        out_shape=jax.ShapeDtypeStruct((n, width), BF16),
        compiler_params=_attn_vmem(seq, tq, tk),
        name="diff_attn",
    )(q, k, vt, lq1.reshape(1, dd), lk1.reshape(1, dd), lq2.reshape(1, dd), lk2.reshape(1, dd),
      subln_g.reshape(HEAD_WIDTH, 1))


def _sb_attn_kernel(q_ref, k_ref, vt_ref, o_ref, *, tq, tk, scale):
    qi = pl.program_id(2)
    q = q_ref[...]
    row = lax.broadcasted_iota(jnp.int32, (tk, tk), 0)
    col = lax.broadcasted_iota(jnp.int32, (tk, tk), 1)
    later = jnp.where(col > row, 1.0, 0.0).astype(BF16)

    def block(kb, carry, diagonal):
        tail, acc = carry
        kblk = k_ref[pl.ds(pl.multiple_of(kb * tk, tk), tk), :]
        z = lax.dot_general(kblk, q, (((1,), (1,)), ((), ())), preferred_element_type=F32) * scale
        log_beta = jnp.minimum(z, 0.0) - jnp.log1p(jnp.exp(-jnp.abs(z)))
        log_1m = log_beta - z
        if diagonal:
            key = kb * tk + lax.broadcasted_iota(jnp.int32, z.shape, 0)
            qry = qi * tq + lax.broadcasted_iota(jnp.int32, z.shape, 1)
            valid = key < qry
            log_1m = jnp.where(valid, log_1m, 0.0)
        hi = log_1m.astype(BF16)
        lo = (log_1m - hi.astype(F32)).astype(BF16)
        survive = (jnp.dot(later, hi, preferred_element_type=F32)
                   + jnp.dot(later, lo, preferred_element_type=F32))
        a = jnp.exp(log_beta + survive + tail)
        if diagonal:
            a = jnp.where(valid, a, 0.0)
        tail = tail + survive[0:1, :] + log_1m[0:1, :]
        acc = acc + jnp.dot(vt_ref[kb], a.astype(BF16), preferred_element_type=F32)
        return tail, acc

    carry = block(qi, (jnp.zeros((1, tq), F32), jnp.zeros((HEAD_WIDTH, tq), F32)), True)
    _, acc = lax.fori_loop(0, qi, lambda t, c: block(qi - 1 - t, c, False), carry)
    o_ref[...] = acc.T.astype(o_ref.dtype)


def _sb_attn(q, k, vt, *, batch, seq, scale):
    n, width = q.shape
    tq, tk = ATTN_TQ, ATTN_TK
    q_spec, k_spec, vt_spec, o_spec = _attn_specs(seq, tq, tk)
    return pl.pallas_call(
        functools.partial(_sb_attn_kernel, tq=tq, tk=tk, scale=scale),
        grid=(batch, width // HEAD_WIDTH, seq // tq),
        in_specs=[q_spec, k_spec, vt_spec],
        out_specs=o_spec,
        out_shape=jax.ShapeDtypeStruct((n, width), BF16),
        compiler_params=_attn_vmem(seq, tq, tk),
        name="sb_attn",
    )(q, k, vt)


def kernel(x, positions, norm_mix_g, norm_ffn_g, norm_final_g, sc_w_in, sc_conv_w, sc_w_out, sg_w_in, sg_ln_g, sg_ln_b, sg_w_s, sg_b_s, sg_w_out, da_w_qkv, da_lambda_q1, da_lambda_k1, da_lambda_q2, da_lambda_k2, da_subln_g, da_w_out, sb_w_qkv, sb_w_out, ffn_w_gate, ffn_w_up, ffn_conv_w, ffn_conv_b, ffn_w_down):
    batch, seq, d = x.shape
    depth = norm_mix_g.shape[0]
    n = batch * seq
    assert seq % ATTN_TQ == 0 and d % HEAD_WIDTH == 0

    def bf(w):
        return w.astype(BF16)

    h = x.reshape(n, d)
    a = _rmsnorm(h, norm_mix_g[0])
    for layer in range(depth):
        mixer, j = layer % 4, layer // 4
        if mixer == 0:
            gb, u = _sc_in(a, bf(sc_w_in[j]))
            y = _sc_conv(u, gb, sc_conv_w[j], seq=seq)
            w_out = sc_w_out[j]
        elif mixer == 1:
            u, v = _sg_in(a, bf(sg_w_in[j]))
            y = _sg_mix(u, v, sg_ln_g[j], sg_ln_b[j], sg_w_s[j], sg_b_s[j])
            w_out = sg_w_out[j]
        elif mixer == 2:
            head_dim = d // DIFF_HEADS // 2
            lambda_init = 0.8 - 0.6 * math.exp(-0.3 * layer)
            tables = _rope_tables(positions, head_dim)
            q, k, vt = _qkv(a, bf(da_w_qkv[j]), tk=ATTN_TK, rope_tables=tables, q_scale=head_dim ** -0.5)
            y = _diff_attn(q, k, vt, da_lambda_q1[j], da_lambda_k1[j], da_lambda_q2[j], da_lambda_k2[j],
                           da_subln_g[j], batch=batch, seq=seq, lambda_init=lambda_init)
            w_out = da_w_out[j]
        else:
            q, k, vt = _qkv(a, bf(sb_w_qkv[j]), tk=ATTN_TK)
            y = _sb_attn(q, k, vt, batch=batch, seq=seq, scale=(d // SB_HEADS) ** -0.5)
            w_out = sb_w_out[j]
        h, a = _out_proj(y, bf(w_out), h, norm_ffn_g[layer])
        t = _ffn_up(a, bf(ffn_w_gate[layer]), bf(ffn_w_up[layer]), ffn_conv_w[layer], ffn_conv_b[layer], seq=seq)
        last = layer == depth - 1
        h, a = _out_proj(t, bf(ffn_w_down[layer]), h, norm_final_g if last else norm_mix_g[layer + 1], final=last)
    return a.reshape(batch, seq, d)
```

```python
import functools
import math

import jax
import jax.numpy as jnp
from jax import lax
from jax.experimental import pallas as pl
from jax.experimental.pallas import tpu as pltpu

F32 = jnp.float32
BF16 = jnp.bfloat16

EPS = 1e-6
ROPE_THETA = 10000.0
CONV_WIDTH = 3
SGU_GROUPS = 16
SGU_CHUNK = 128
DIFF_HEADS = 16
SB_HEADS = 16
HEAD_WIDTH = 128

V7X_LANES = 128
V7X_SUBLANES = 8
V7X_VMEM_BYTES = 64 * 1024 * 1024
INTERNAL_SCRATCH_BYTES = 12 * 1024 * 1024


def _nbytes(shape, dtype):
    return math.prod(shape) * jnp.dtype(dtype).itemsize


def _params(semantics, pipelined_bytes, resident_bytes=0):
    limit = 2 * pipelined_bytes + resident_bytes + INTERNAL_SCRATCH_BYTES
    limit = min(limit, V7X_VMEM_BYTES - 4 * 1024 * 1024)
    return pltpu.CompilerParams(dimension_semantics=semantics, vmem_limit_bytes=int(limit))


def _row_tile(n, want):
    t = min(n, want)
    assert n % t == 0, (n, t)
    return t


def _rms(x, g):
    ms = jnp.mean(x * x, axis=-1, keepdims=True)
    return x * lax.rsqrt(ms + EPS) * g


def _shift_rows(x, prev, k):
    row = lax.broadcasted_iota(jnp.int32, x.shape, 0)
    out = pltpu.roll(x, k, 0)
    for r in range(k):
        out = jnp.where(row == r, prev[V7X_SUBLANES - k + r:V7X_SUBLANES - k + r + 1, :], out)
    return out


def _rmsnorm_kernel(x_ref, g_ref, o_ref):
    o_ref[...] = _rms(x_ref[...], g_ref[...]).astype(o_ref.dtype)


def _rmsnorm(h, g):
    n, d = h.shape
    tm = _row_tile(n, 512)
    return pl.pallas_call(
        _rmsnorm_kernel,
        grid=(n // tm,),
        in_specs=[pl.BlockSpec((tm, d), lambda i: (i, 0)), pl.BlockSpec((1, d), lambda i: (0, 0))],
        out_specs=pl.BlockSpec((tm, d), lambda i: (i, 0)),
        out_shape=jax.ShapeDtypeStruct((n, d), BF16),
        compiler_params=_params(("arbitrary",), _nbytes((tm, d), F32) + _nbytes((tm, d), BF16)),
        name="rmsnorm",
    )(h, g.reshape(1, d))


def _out_proj_kernel(x_ref, w_ref, h_ref, g_ref, *out_refs, emit_h):
    hn = h_ref[...] + jnp.dot(x_ref[...], w_ref[...], preferred_element_type=F32)
    if emit_h:
        out_refs[0][...] = hn
    an_ref = out_refs[-1]
    an_ref[...] = _rms(hn, g_ref[...]).astype(an_ref.dtype)


def _out_proj(x, w, h, g, *, final=False):
    n, k = x.shape
    d = w.shape[1]
    tm = _row_tile(n, 256)
    an_dtype = F32 if final else BF16
    out_shape = [jax.ShapeDtypeStruct((n, d), an_dtype)]
    out_specs = [pl.BlockSpec((tm, d), lambda i: (i, 0))]
    if not final:
        out_shape.insert(0, jax.ShapeDtypeStruct((n, d), F32))
        out_specs.insert(0, pl.BlockSpec((tm, d), lambda i: (i, 0)))
    pipelined = (_nbytes((tm, k), BF16) + 2 * _nbytes((tm, d), F32) + _nbytes((tm, d), an_dtype))
    outs = pl.pallas_call(
        functools.partial(_out_proj_kernel, emit_h=not final),
        grid=(n // tm,),
        in_specs=[
            pl.BlockSpec((tm, k), lambda i: (i, 0)),
            pl.BlockSpec((k, d), lambda i: (0, 0), pipeline_mode=pl.Buffered(1)),
            pl.BlockSpec((tm, d), lambda i: (i, 0)),
            pl.BlockSpec((1, d), lambda i: (0, 0)),
        ],
        out_specs=out_specs,
        out_shape=out_shape,
        compiler_params=_params(("arbitrary",), pipelined, _nbytes((k, d), BF16)),
        name="out_proj",
    )(x, w, h, g.reshape(1, d))
    return (None, outs[0]) if final else (outs[0], outs[1])


def _ffn_up_kernel(a_ref, wg_ref, wu_ref, cw_ref, cb_ref, t_ref, carry_ref, *, tm, seq):
    i = pl.program_id(1)

    @pl.when((i * tm) % seq == 0)
    def _():
        carry_ref[...] = jnp.zeros_like(carry_ref)

    a = a_ref[...]
    g = jnp.dot(a, wg_ref[...], preferred_element_type=F32)
    up = jnp.dot(a, wu_ref[...], preferred_element_type=F32)
    prev = carry_ref[...]
    cw = cw_ref[...]
    conv = (cw[0:1, :] * _shift_rows(g, prev, 2) + cw[1:2, :] * _shift_rows(g, prev, 1)
            + cw[2:3, :] * g + cb_ref[...])
    t_ref[...] = (conv * jax.nn.sigmoid(conv) * up).astype(t_ref.dtype)
    carry_ref[...] = g[tm - V7X_SUBLANES:, :]


def _ffn_up(a, wg, wu, conv_w, conv_b, *, seq):
    n, d = a.shape
    f = wg.shape[1]
    tm = _row_tile(seq, 1024)
    tn = 512
    assert f % tn == 0
    pipelined = (_nbytes((tm, d), BF16) + 2 * _nbytes((d, tn), BF16) + _nbytes((tm, tn), BF16)
                 + 2 * _nbytes((tm, tn), F32))
    return pl.pallas_call(
        functools.partial(_ffn_up_kernel, tm=tm, seq=seq),
        grid=(f // tn, n // tm),
        in_specs=[
            pl.BlockSpec((tm, d), lambda j, i: (i, 0)),
            pl.BlockSpec((d, tn), lambda j, i: (0, j)),
            pl.BlockSpec((d, tn), lambda j, i: (0, j)),
            pl.BlockSpec((CONV_WIDTH, tn), lambda j, i: (0, j)),
            pl.BlockSpec((1, tn), lambda j, i: (0, j)),
        ],
        out_specs=pl.BlockSpec((tm, tn), lambda j, i: (i, j)),
        out_shape=jax.ShapeDtypeStruct((n, f), BF16),
        scratch_shapes=[pltpu.VMEM((V7X_SUBLANES, tn), F32)],
        compiler_params=_params(("arbitrary", "arbitrary"), pipelined),
        name="ffn_up",
    )(a, wg, wu, conv_w, conv_b.reshape(1, f))


def _sc_in_kernel(a_ref, wb_ref, wc_ref, wx_ref, gb_ref, u_ref):
    a = a_ref[...]
    gb_ref[...] = jnp.dot(a, wb_ref[...], preferred_element_type=F32)
    gc = jnp.dot(a, wc_ref[...], preferred_element_type=F32)
    xi = jnp.dot(a, wx_ref[...], preferred_element_type=F32)
    u_ref[...] = gc * xi


def _sc_in(a, w_in):
    n, d = a.shape
    tm = _row_tile(n, 1024)
    tn = 512
    nj = d // tn
    pipelined = _nbytes((tm, d), BF16) + 3 * _nbytes((d, tn), BF16) + 3 * _nbytes((tm, tn), F32)
    return pl.pallas_call(
        _sc_in_kernel,
        grid=(nj, n // tm),
        in_specs=[
            pl.BlockSpec((tm, d), lambda j, i: (i, 0)),
            pl.BlockSpec((d, tn), lambda j, i: (0, j)),
            pl.BlockSpec((d, tn), lambda j, i: (0, j + nj)),
            pl.BlockSpec((d, tn), lambda j, i: (0, j + 2 * nj)),
        ],
        out_specs=[pl.BlockSpec((tm, tn), lambda j, i: (i, j))] * 2,
        out_shape=[jax.ShapeDtypeStruct((n, d), F32)] * 2,
        compiler_params=_params(("arbitrary", "arbitrary"), pipelined),
        name="sc_in",
    )(a, w_in, w_in, w_in)


def _sc_conv_kernel(u_ref, halo_ref, gb_ref, cw_ref, y_ref, *, tm, seq):
    i = pl.program_id(0)
    u = u_ref[...]
    prev = jnp.where((i * tm) % seq == 0, 0.0, halo_ref[...])
    cw = cw_ref[...]
    conv = cw[0:1, :] * _shift_rows(u, prev, 2) + cw[1:2, :] * _shift_rows(u, prev, 1) + cw[2:3, :] * u
    y_ref[...] = (gb_ref[...] * conv).astype(y_ref.dtype)


def _sc_conv(u, gb, conv_w, *, seq):
    n, d = u.shape
    tm = _row_tile(seq, 512)
    hb = tm // V7X_SUBLANES
    pipelined = 2 * _nbytes((tm, d), F32) + _nbytes((tm, d), BF16) + _nbytes((tm, d), F32)
    return pl.pallas_call(
        functools.partial(_sc_conv_kernel, tm=tm, seq=seq),
        grid=(n // tm,),
        in_specs=[
            pl.BlockSpec((tm, d), lambda i: (i, 0)),
            pl.BlockSpec((V7X_SUBLANES, d), lambda i: (jnp.maximum(i * hb - 1, 0), 0)),
            pl.BlockSpec((tm, d), lambda i: (i, 0)),
            pl.BlockSpec((CONV_WIDTH, d), lambda i: (0, 0)),
        ],
        out_specs=pl.BlockSpec((tm, d), lambda i: (i, 0)),
        out_shape=jax.ShapeDtypeStruct((n, d), BF16),
        compiler_params=_params(("arbitrary",), pipelined),
        name="sc_conv",
    )(u, u, gb, conv_w)


def _sg_in_kernel(a_ref, wu_ref, wv_ref, u_ref, v_ref):
    a = a_ref[...]
    u_ref[...] = jax.nn.gelu(jnp.dot(a, wu_ref[...], preferred_element_type=F32))
    v_ref[...] = jax.nn.gelu(jnp.dot(a, wv_ref[...], preferred_element_type=F32))


def _sg_in(a, w_in):
    n, d = a.shape
    width = w_in.shape[1] // 2
    tm = _row_tile(n, 1024)
    tn = 512
    nj = width // tn
    pipelined = _nbytes((tm, d), BF16) + 2 * _nbytes((d, tn), BF16) + 2 * _nbytes((tm, tn), F32)
    return pl.pallas_call(
        _sg_in_kernel,
        grid=(nj, n // tm),
        in_specs=[
            pl.BlockSpec((tm, d), lambda j, i: (i, 0)),
            pl.BlockSpec((d, tn), lambda j, i: (0, j)),
            pl.BlockSpec((d, tn), lambda j, i: (0, j + nj)),
        ],
        out_specs=[pl.BlockSpec((tm, tn), lambda j, i: (i, j))] * 2,
        out_shape=[jax.ShapeDtypeStruct((n, width), F32)] * 2,
        compiler_params=_params(("arbitrary", "arbitrary"), pipelined),
        name="sg_in",
    )(a, w_in, w_in)


def _sg_mix_kernel(u_ref, v_ref, lng_ref, lnb_ref, ws_ref, bs_ref, y_ref, vn_ref, *, tm):
    v = v_ref[...]
    mu = jnp.mean(v, axis=-1, keepdims=True)
    vc = v - mu
    var = jnp.mean(vc * vc, axis=-1, keepdims=True)
    vn_ref[...] = (vc * lax.rsqrt(var + EPS) * lng_ref[...] + lnb_ref[...]).astype(vn_ref.dtype)
    t = SGU_CHUNK
    nchunk = tm // t
    row = lax.broadcasted_iota(jnp.int32, (t, t), 0)
    col = lax.broadcasted_iota(jnp.int32, (t, t), 1)
    bs = bs_ref[...]
    for g in range(SGU_GROUPS):
        cols = slice(g * t, (g + 1) * t)
        w = jnp.where(row >= col, ws_ref[g], 0.0).astype(BF16)
        rhs = jnp.concatenate([vn_ref[c * t:(c + 1) * t, cols] for c in range(nchunk)], axis=1)
        mixed = jnp.dot(w, rhs, preferred_element_type=F32) + bs[:, g:g + 1]
        for c in range(nchunk):
            rows = slice(c * t, (c + 1) * t)
            y_ref[rows, cols] = (u_ref[rows, cols] * mixed[:, c * t:(c + 1) * t]).astype(y_ref.dtype)


def _sg_mix(u, v, ln_g, ln_b, w_s, b_s):
    n, width = u.shape
    tm = _row_tile(n, 512)
    assert tm % SGU_CHUNK == 0
    pipelined = 2 * _nbytes((tm, width), F32) + _nbytes((tm, width), BF16)
    resident = 2 * _nbytes(w_s.shape, F32) + _nbytes((tm, width), BF16)
    return pl.pallas_call(
        functools.partial(_sg_mix_kernel, tm=tm),
        grid=(n // tm,),
        in_specs=[
            pl.BlockSpec((tm, width), lambda i: (i, 0)),
            pl.BlockSpec((tm, width), lambda i: (i, 0)),
            pl.BlockSpec((1, width), lambda i: (0, 0)),
            pl.BlockSpec((1, width), lambda i: (0, 0)),
            pl.BlockSpec(w_s.shape, lambda i: (0, 0, 0)),
            pl.BlockSpec((SGU_CHUNK, SGU_GROUPS), lambda i: (0, 0)),
        ],
        out_specs=pl.BlockSpec((tm, width), lambda i: (i, 0)),
        out_shape=jax.ShapeDtypeStruct((n, width), BF16),
        scratch_shapes=[pltpu.VMEM((tm, width), BF16)],
        compiler_params=_params(("arbitrary",), pipelined, resident),
        name="sg_mix",
    )(u, v, ln_g.reshape(1, width), ln_b.reshape(1, width), w_s, b_s.T)


ATTN_TQ = 256
DIFF_TK = 512
DIFF_HEADS_PER_STEP = 2
SB_TK = 256
SB_HEADS_PER_STEP = 4
LOG2E = 1.4426950408889634


def _rope_table_kernel(pos_ref, invf_ref, sign_ref, cos_ref, sin_ref):
    ang = pos_ref[...] * invf_ref[...]
    cos_ref[...] = jnp.cos(ang)
    sin_ref[...] = jnp.sin(ang) * sign_ref[...]


def _rope_tables(positions, head_dim):
    n = positions.size
    half = head_dim // 2
    inv_freq = 1.0 / (ROPE_THETA ** (jnp.arange(0, head_dim, 2, dtype=F32) / head_dim))
    lane = jnp.arange(V7X_LANES)
    invf = inv_freq[lane % half].reshape(1, V7X_LANES)
    sign = jnp.where(lane % head_dim < half, -1.0, 1.0).astype(F32).reshape(1, V7X_LANES)
    tm = _row_tile(n, 1024)
    return pl.pallas_call(
        _rope_table_kernel,
        grid=(n // tm,),
        in_specs=[
            pl.BlockSpec((tm, 1), lambda i: (i, 0)),
            pl.BlockSpec((1, V7X_LANES), lambda i: (0, 0)),
            pl.BlockSpec((1, V7X_LANES), lambda i: (0, 0)),
        ],
        out_specs=[pl.BlockSpec((tm, V7X_LANES), lambda i: (i, 0))] * 2,
        out_shape=[jax.ShapeDtypeStruct((n, V7X_LANES), F32)] * 2,
        compiler_params=_params(("arbitrary",), 3 * _nbytes((tm, V7X_LANES), F32)),
        name="rope_tables",
    )(positions.astype(F32).reshape(n, 1), invf, sign)


def _rotate_half(x, cos, sin, first_half):
    partner = jnp.where(first_half, pltpu.roll(x, 96, 1), pltpu.roll(x, 32, 1))
    return x * cos + partner * sin


def _qkv_kernel(a_ref, wq_ref, wk_ref, wv_ref, *refs, rope, q_scale, tk):
    if rope:
        cos_ref, sin_ref, q_ref, k_ref, vt_ref = refs
    else:
        q_ref, k_ref, vt_ref = refs
    a = a_ref[...]
    q = jnp.dot(a, wq_ref[...], preferred_element_type=F32)
    k = jnp.dot(a, wk_ref[...], preferred_element_type=F32)
    v = jnp.dot(a, wv_ref[...], preferred_element_type=F32)
    tm, tn = q.shape
    if rope:
        cos = cos_ref[...]
        sin = sin_ref[...]
        lane = lax.broadcasted_iota(jnp.int32, cos.shape, 1)
        first_half = (lane % 64) < 32
        for c in range(tn // V7X_LANES):
            cols = slice(c * V7X_LANES, (c + 1) * V7X_LANES)
            q_ref[:, cols] = (_rotate_half(q[:, cols], cos, sin, first_half) * q_scale).astype(q_ref.dtype)
            k_ref[:, cols] = _rotate_half(k[:, cols], cos, sin, first_half).astype(k_ref.dtype)
    else:
        q_ref[...] = q.astype(q_ref.dtype)
        k_ref[...] = k.astype(k_ref.dtype)
    for hh in range(tn // HEAD_WIDTH):
        for kk in range(tm // tk):
            blk = v[kk * tk:(kk + 1) * tk, hh * HEAD_WIDTH:(hh + 1) * HEAD_WIDTH]
            vt_ref[hh, kk] = blk.T.astype(vt_ref.dtype)


def _qkv(a, w_qkv, *, tk, rope_tables=None, q_scale=1.0):
    n, d = a.shape
    width = w_qkv.shape[1] // 3
    tm = _row_tile(n, 1024)
    tn = 512
    nj = width // tn
    rope = rope_tables is not None
    in_specs = [
        pl.BlockSpec((tm, d), lambda j, i: (i, 0)),
        pl.BlockSpec((d, tn), lambda j, i: (0, j)),
        pl.BlockSpec((d, tn), lambda j, i: (0, j + nj)),
        pl.BlockSpec((d, tn), lambda j, i: (0, j + 2 * nj)),
    ]
    args = [a, w_qkv, w_qkv, w_qkv]
    if rope:
        in_specs += [pl.BlockSpec((tm, V7X_LANES), lambda j, i: (i, 0))] * 2
        args += list(rope_tables)
    heads = width // HEAD_WIDTH
    pipelined = (_nbytes((tm, d), BF16) + 3 * _nbytes((d, tn), BF16) + 3 * _nbytes((tm, tn), BF16)
                 + 3 * _nbytes((tm, tn), F32) + 2 * _nbytes((tm, V7X_LANES), F32))
    return pl.pallas_call(
        functools.partial(_qkv_kernel, rope=rope, q_scale=q_scale, tk=tk),
        grid=(nj, n // tm),
        in_specs=in_specs,
        out_specs=[
            pl.BlockSpec((tm, tn), lambda j, i: (i, j)),
            pl.BlockSpec((tm, tn), lambda j, i: (i, j)),
            pl.BlockSpec((tn // HEAD_WIDTH, tm // tk, HEAD_WIDTH, tk), lambda j, i: (j, i, 0, 0)),
        ],
        out_shape=[
            jax.ShapeDtypeStruct((n, width), BF16),
            jax.ShapeDtypeStruct((n, width), BF16),
            jax.ShapeDtypeStruct((heads, n // tk, HEAD_WIDTH, tk), BF16),
        ],
        compiler_params=_params(("arbitrary", "arbitrary"), pipelined),
        name="qkv_rope" if rope else "qkv",
    )(*args)


def _attn_specs(seq, tq, tk, heads):
    nq = seq // tq
    nk = seq // tk
    w = heads * HEAD_WIDTH
    q_spec = pl.BlockSpec((tq, w), lambda b, h, qi: (b * nq + qi, h))
    k_spec = pl.BlockSpec((seq, w), lambda b, h, qi: (b, h))
    vt_spec = pl.BlockSpec((heads, nk, HEAD_WIDTH, tk), lambda b, h, qi: (h, b, 0, 0))
    o_spec = pl.BlockSpec((tq, w), lambda b, h, qi: (b * nq + qi, h))
    return q_spec, k_spec, vt_spec, o_spec


def _attn_params(seq, tq, tk, heads, score_cols):
    w = heads * HEAD_WIDTH
    pipelined = 2 * _nbytes((tq, w), BF16) + 2 * _nbytes((seq, w), BF16)
    scratch = (_nbytes((2, heads, tk, score_cols), F32) + _nbytes((2, heads, tk, score_cols), BF16)
               + _nbytes((heads, HEAD_WIDTH, score_cols), F32))
    return _params(("arbitrary", "arbitrary", "arbitrary"), pipelined, scratch)


def _head_cols(hh):
    return slice(hh * HEAD_WIDTH, (hh + 1) * HEAD_WIDTH)


def _diff_attn_kernel(q_ref, k_ref, vt_ref, lq1_ref, lk1_ref, lq2_ref, lk2_ref, g_ref, o_ref,
                      qq_ref, s_ref, p_ref, alpha_ref, m_ref, l_ref, acc_ref, *,
                      tq, tk, heads, lambda_init):
    qi = pl.program_id(2)
    lane = lax.broadcasted_iota(jnp.int32, (tq, HEAD_WIDTH), 1)
    p_ref[...] = jnp.zeros_like(p_ref)
    alpha_ref[...] = jnp.ones_like(alpha_ref)
    for hh in range(heads):
        q = q_ref[:, _head_cols(hh)]
        zero = jnp.zeros_like(q)
        qq_ref[hh, :tq, :] = jnp.where(lane < 64, q, zero)
        qq_ref[hh, tq:, :] = jnp.where(lane >= 64, q, zero)
        m_ref[hh] = jnp.full(m_ref.shape[1:], -jnp.inf, F32)
        l_ref[hh] = jnp.zeros(l_ref.shape[1:], F32)
        acc_ref[hh] = jnp.zeros(acc_ref.shape[1:], F32)

    chunks = [(hh, slice(c * tq, (c + 1) * tq)) for hh in range(heads) for c in range(2)]

    def scores_to(slot, kb, hh, cols):
        rows = pl.ds(pl.multiple_of(kb * tk, tk), tk)
        s_ref[slot, hh, :, cols] = lax.dot_general(
            k_ref[rows, _head_cols(hh)], qq_ref[hh, cols, :], (((1,), (1,)), ((), ())),
            preferred_element_type=F32)

    def accumulate(slot, kb, hh, cols):
        acc_ref[hh, :, cols] = alpha_ref[slot, hh, :, cols] * acc_ref[hh, :, cols] + jnp.dot(
            vt_ref[hh, kb], p_ref[slot, hh, :, cols], preferred_element_type=F32)

    def softmax_to(slot, kb, hh, cols, diagonal):
        s = s_ref[slot, hh, :, cols]
        if diagonal:
            key = kb * tk + lax.broadcasted_iota(jnp.int32, s.shape, 0)
            qry = qi * tq + lax.broadcasted_iota(jnp.int32, s.shape, 1)
            s = jnp.where(key <= qry, s, -jnp.inf)
        m = m_ref[hh, :, cols]
        m_new = jnp.maximum(m, jnp.max(s, axis=0, keepdims=True))
        alpha = jnp.exp2(m - m_new)
        p = jnp.exp2(s - m_new)
        m_ref[hh, :, cols] = m_new
        l_ref[hh, :, cols] = alpha * l_ref[hh, :, cols] + jnp.sum(p, axis=0, keepdims=True)
        alpha_ref[slot, hh, :, cols] = alpha
        p_ref[slot, hh, :, cols] = p.astype(BF16)

    def step(slot, kb, diagonal):
        other = 1 - slot
        for hh, cols in chunks:
            accumulate(other, jnp.maximum(kb - 1, 0), hh, cols)
            if not diagonal:
                scores_to(other, kb + 1, hh, cols)
            softmax_to(slot, kb, hh, cols, diagonal)

    nfull = (qi * tq) // tk
    @pl.when(nfull % 2 == 0)
    def _():
        for hh, cols in chunks:
            scores_to(0, 0, hh, cols)

    @pl.when(nfull % 2 == 1)
    def _():
        for hh, cols in chunks:
            scores_to(1, 0, hh, cols)

    def body(kb, _):
        @pl.when((nfull - kb) % 2 == 0)
        def _():
            step(0, kb, False)

        @pl.when((nfull - kb) % 2 == 1)
        def _():
            step(1, kb, False)

        return 0

    lax.fori_loop(0, nfull, body, 0)
    step(0, nfull, True)
    for hh, cols in chunks:
        accumulate(0, nfull, hh, cols)

    lam = (jnp.exp(jnp.sum(lq1_ref[...] * lk1_ref[...], axis=-1, keepdims=True))
           - jnp.exp(jnp.sum(lq2_ref[...] * lk2_ref[...], axis=-1, keepdims=True)) + lambda_init)
    for hh in range(heads):
        o = acc_ref[hh] / l_ref[hh]
        o = o[:, :tq] - lam * o[:, tq:]
        ms = jnp.mean(o * o, axis=0, keepdims=True)
        o = o * lax.rsqrt(ms + EPS) * g_ref[...] * (1.0 - lambda_init)
        o_ref[:, _head_cols(hh)] = o.T.astype(o_ref.dtype)


def _diff_attn(q, k, vt, lq1, lk1, lq2, lk2, subln_g, *, batch, seq, lambda_init):
    n, width = q.shape
    tq, tk, heads = ATTN_TQ, DIFF_TK, DIFF_HEADS_PER_STEP
    q_spec, k_spec, vt_spec, o_spec = _attn_specs(seq, tq, tk, heads)
    dd = lq1.shape[0]
    small = pl.BlockSpec((1, dd), lambda b, h, qi: (0, 0))
    return pl.pallas_call(
        functools.partial(_diff_attn_kernel, tq=tq, tk=tk, heads=heads, lambda_init=lambda_init),
        grid=(batch, width // (heads * HEAD_WIDTH), seq // tq),
        in_specs=[q_spec, k_spec, vt_spec, small, small, small, small,
                  pl.BlockSpec((HEAD_WIDTH, 1), lambda b, h, qi: (0, 0))],
        out_specs=o_spec,
        out_shape=jax.ShapeDtypeStruct((n, width), BF16),
        scratch_shapes=[
            pltpu.VMEM((heads, 2 * tq, HEAD_WIDTH), BF16),
            pltpu.VMEM((2, heads, tk, 2 * tq), F32),
            pltpu.VMEM((2, heads, tk, 2 * tq), BF16),
            pltpu.VMEM((2, heads, 1, 2 * tq), F32),
            pltpu.VMEM((heads, 1, 2 * tq), F32),
            pltpu.VMEM((heads, 1, 2 * tq), F32),
            pltpu.VMEM((heads, HEAD_WIDTH, 2 * tq), F32),
        ],
        compiler_params=_attn_params(seq, tq, tk, heads, 2 * tq),
        name="diff_attn",
    )(q, k, vt, lq1.reshape(1, dd), lk1.reshape(1, dd), lq2.reshape(1, dd), lk2.reshape(1, dd),
      subln_g.reshape(HEAD_WIDTH, 1))


def _sb_attn_kernel(q_ref, k_ref, vt_ref, o_ref, later_ref, z_ref, tail_ref, acc_ref, *,
                    tq, tk, heads, scale):
    qi = pl.program_id(2)
    row = lax.broadcasted_iota(jnp.int32, (tk, tk), 0)
    col = lax.broadcasted_iota(jnp.int32, (tk, tk), 1)
    later_ref[...] = jnp.where(col > row, -1.0, 0.0).astype(BF16)
    tail_ref[...] = jnp.zeros_like(tail_ref)
    acc_ref[...] = jnp.zeros_like(acc_ref)

    def scores_to(slot, kb):
        rows = pl.ds(pl.multiple_of(kb * tk, tk), tk)
        for hh in range(heads):
            z_ref[slot, hh] = lax.dot_general(
                k_ref[rows, _head_cols(hh)], q_ref[:, _head_cols(hh)], (((1,), (1,)), ((), ())),
                preferred_element_type=F32) * (scale * LOG2E)

    def block(slot, kb, diagonal):
        if diagonal:
            key = kb * tk + lax.broadcasted_iota(jnp.int32, (tk, tq), 0)
            qry = qi * tq + lax.broadcasted_iota(jnp.int32, (tk, tq), 1)
            valid = key < qry
        neg_later = later_ref[...]
        first_rows, his, los = [], [], []
        for hh in range(heads):
            z = z_ref[slot, hh]
            softplus = jnp.maximum(z, 0.0) + jnp.log2(1.0 + jnp.exp2(-jnp.abs(z)))
            z_ref[slot, hh] = z - softplus
            if diagonal:
                softplus = jnp.where(valid, softplus, 0.0)
            hi = softplus.astype(BF16)
            first_rows.append(softplus[0:1, :])
            his.append(hi)
            los.append((softplus - hi.astype(F32)).astype(BF16))
        survives = [jnp.dot(neg_later, his[hh], preferred_element_type=F32)
                    + jnp.dot(neg_later, los[hh], preferred_element_type=F32) for hh in range(heads)]
        weights = []
        for hh in range(heads):
            tail = tail_ref[hh]
            a = jnp.exp2(z_ref[slot, hh] + survives[hh] + tail)
            if diagonal:
                a = jnp.where(valid, a, 0.0)
            weights.append(a.astype(BF16))
            tail_ref[hh] = tail + survives[hh][0:1, :] - first_rows[hh]
        for hh in range(heads):
            acc_ref[hh] += jnp.dot(vt_ref[hh, kb], weights[hh], preferred_element_type=F32)

    scores_to(0, qi)
    scores_to(1, jnp.maximum(qi - 1, 0))
    block(0, qi, True)

    def body(t, _):
        kb = qi - 1 - t

        @pl.when(t % 2 == 0)
        def _():
            scores_to(0, jnp.maximum(kb - 1, 0))
            block(1, kb, False)

        @pl.when(t % 2 == 1)
        def _():
            scores_to(1, jnp.maximum(kb - 1, 0))
            block(0, kb, False)

        return 0

    lax.fori_loop(0, qi, body, 0)
    for hh in range(heads):
        o_ref[:, _head_cols(hh)] = acc_ref[hh].T.astype(o_ref.dtype)


def _sb_attn(q, k, vt, *, batch, seq, scale):
    n, width = q.shape
    tq, tk, heads = ATTN_TQ, SB_TK, SB_HEADS_PER_STEP
    assert tq == tk
    q_spec, k_spec, vt_spec, o_spec = _attn_specs(seq, tq, tk, heads)
    return pl.pallas_call(
        functools.partial(_sb_attn_kernel, tq=tq, tk=tk, heads=heads, scale=scale),
        grid=(batch, width // (heads * HEAD_WIDTH), seq // tq),
        in_specs=[q_spec, k_spec, vt_spec],
        out_specs=o_spec,
        out_shape=jax.ShapeDtypeStruct((n, width), BF16),
        scratch_shapes=[
            pltpu.VMEM((tk, tk), BF16),
            pltpu.VMEM((2, heads, tk, tq), F32),
            pltpu.VMEM((heads, 1, tq), F32),
            pltpu.VMEM((heads, HEAD_WIDTH, tq), F32),
        ],
        compiler_params=_attn_params(seq, tq, tk, heads, tq),
        name="sb_attn",
    )(q, k, vt)


def kernel(x, positions, norm_mix_g, norm_ffn_g, norm_final_g, sc_w_in, sc_conv_w, sc_w_out, sg_w_in, sg_ln_g, sg_ln_b, sg_w_s, sg_b_s, sg_w_out, da_w_qkv, da_lambda_q1, da_lambda_k1, da_lambda_q2, da_lambda_k2, da_subln_g, da_w_out, sb_w_qkv, sb_w_out, ffn_w_gate, ffn_w_up, ffn_conv_w, ffn_conv_b, ffn_w_down):
    batch, seq, d = x.shape
    depth = norm_mix_g.shape[0]
    n = batch * seq
    assert seq % DIFF_TK == 0 and seq % ATTN_TQ == 0 and d % HEAD_WIDTH == 0

    def bf(w):
        return w.astype(BF16)

    h = x.reshape(n, d)
    a = _rmsnorm(h, norm_mix_g[0])
    for layer in range(depth):
        mixer, j = layer % 4, layer // 4
        if mixer == 0:
            gb, u = _sc_in(a, bf(sc_w_in[j]))
            y = _sc_conv(u, gb, sc_conv_w[j], seq=seq)
            w_out = sc_w_out[j]
        elif mixer == 1:
            u, v = _sg_in(a, bf(sg_w_in[j]))
            y = _sg_mix(u, v, sg_ln_g[j], sg_ln_b[j], sg_w_s[j], sg_b_s[j])
            w_out = sg_w_out[j]
        elif mixer == 2:
            head_dim = d // DIFF_HEADS // 2
            lambda_init = 0.8 - 0.6 * math.exp(-0.3 * layer)
            tables = _rope_tables(positions, head_dim)
            q, k, vt = _qkv(a, bf(da_w_qkv[j]), tk=DIFF_TK, rope_tables=tables,
                            q_scale=head_dim ** -0.5 * LOG2E)
            y = _diff_attn(q, k, vt, da_lambda_q1[j], da_lambda_k1[j], da_lambda_q2[j], da_lambda_k2[j],
                           da_subln_g[j], batch=batch, seq=seq, lambda_init=lambda_init)
            w_out = da_w_out[j]
        else:
            q, k, vt = _qkv(a, bf(sb_w_qkv[j]), tk=SB_TK)
            y = _sb_attn(q, k, vt, batch=batch, seq=seq, scale=(d // SB_HEADS) ** -0.5)
            w_out = sb_w_out[j]
        h, a = _out_proj(y, bf(w_out), h, norm_ffn_g[layer])
        t = _ffn_up(a, bf(ffn_w_gate[layer]), bf(ffn_w_up[layer]), ffn_conv_w[layer], ffn_conv_b[layer], seq=seq)
        last = layer == depth - 1
        h, a = _out_proj(t, bf(ffn_w_down[layer]), h, norm_final_g if last else norm_mix_g[layer + 1], final=last)
    return a.reshape(batch, seq, d)
```

```python
import functools
import math

import jax
import jax.numpy as jnp
from jax import lax
from jax.experimental import pallas as pl
from jax.experimental.pallas import tpu as pltpu

F32 = jnp.float32
BF16 = jnp.bfloat16

EPS = 1e-6
ROPE_THETA = 10000.0
CONV_WIDTH = 3
SGU_GROUPS = 16
SGU_CHUNK = 128
DIFF_HEADS = 16
SB_HEADS = 16
HEAD_WIDTH = 128

V7X_LANES = 128
V7X_SUBLANES = 8
V7X_VMEM_BYTES = 64 * 1024 * 1024
INTERNAL_SCRATCH_BYTES = 12 * 1024 * 1024


def _nbytes(shape, dtype):
    return math.prod(shape) * jnp.dtype(dtype).itemsize


def _params(semantics, pipelined_bytes, resident_bytes=0):
    limit = 2 * pipelined_bytes + resident_bytes + INTERNAL_SCRATCH_BYTES
    limit = min(limit, V7X_VMEM_BYTES - 4 * 1024 * 1024)
    return pltpu.CompilerParams(dimension_semantics=semantics, vmem_limit_bytes=int(limit))


def _row_tile(n, want):
    t = min(n, want)
    assert n % t == 0, (n, t)
    return t


def _rms(x, g):
    ms = jnp.mean(x * x, axis=-1, keepdims=True)
    return x * lax.rsqrt(ms + EPS) * g


def _shift_rows(x, prev, k):
    row = lax.broadcasted_iota(jnp.int32, x.shape, 0)
    out = pltpu.roll(x, k, 0)
    for r in range(k):
        out = jnp.where(row == r, prev[V7X_SUBLANES - k + r:V7X_SUBLANES - k + r + 1, :], out)
    return out


def _rmsnorm_kernel(x_ref, g_ref, o_ref):
    o_ref[...] = _rms(x_ref[...], g_ref[...]).astype(o_ref.dtype)


def _rmsnorm(h, g):
    n, d = h.shape
    tm = _row_tile(n, 512)
    return pl.pallas_call(
        _rmsnorm_kernel,
        grid=(n // tm,),
        in_specs=[pl.BlockSpec((tm, d), lambda i: (i, 0)), pl.BlockSpec((1, d), lambda i: (0, 0))],
        out_specs=pl.BlockSpec((tm, d), lambda i: (i, 0)),
        out_shape=jax.ShapeDtypeStruct((n, d), BF16),
        compiler_params=_params(("arbitrary",), _nbytes((tm, d), F32) + _nbytes((tm, d), BF16)),
        name="rmsnorm",
    )(h, g.reshape(1, d))


def _out_proj_kernel(x_ref, w_ref, h_ref, g_ref, *out_refs, emit_h):
    hn = h_ref[...] + jnp.dot(x_ref[...], w_ref[...], preferred_element_type=F32)
    if emit_h:
        out_refs[0][...] = hn
    an_ref = out_refs[-1]
    an_ref[...] = _rms(hn, g_ref[...]).astype(an_ref.dtype)


def _out_proj(x, w, h, g, *, final=False):
    n, k = x.shape
    d = w.shape[1]
    tm = _row_tile(n, 256)
    an_dtype = F32 if final else BF16
    out_shape = [jax.ShapeDtypeStruct((n, d), an_dtype)]
    out_specs = [pl.BlockSpec((tm, d), lambda i: (i, 0))]
    if not final:
        out_shape.insert(0, jax.ShapeDtypeStruct((n, d), F32))
        out_specs.insert(0, pl.BlockSpec((tm, d), lambda i: (i, 0)))
    pipelined = (_nbytes((tm, k), BF16) + 2 * _nbytes((tm, d), F32) + _nbytes((tm, d), an_dtype))
    outs = pl.pallas_call(
        functools.partial(_out_proj_kernel, emit_h=not final),
        grid=(n // tm,),
        in_specs=[
            pl.BlockSpec((tm, k), lambda i: (i, 0)),
            pl.BlockSpec((k, d), lambda i: (0, 0), pipeline_mode=pl.Buffered(1)),
            pl.BlockSpec((tm, d), lambda i: (i, 0)),
            pl.BlockSpec((1, d), lambda i: (0, 0)),
        ],
        out_specs=out_specs,
        out_shape=out_shape,
        compiler_params=_params(("arbitrary",), pipelined, _nbytes((k, d), BF16)),
        name="out_proj",
    )(x, w, h, g.reshape(1, d))
    return (None, outs[0]) if final else (outs[0], outs[1])


def _ffn_up_kernel(a_ref, wg_ref, wu_ref, cw_ref, cb_ref, t_ref, carry_ref, wg_bf_ref, wu_bf_ref, *,
                   tm, seq):
    i = pl.program_id(1)

    @pl.when(i == 0)
    def _():
        wg_bf_ref[...] = wg_ref[...].astype(BF16)
        wu_bf_ref[...] = wu_ref[...].astype(BF16)

    @pl.when((i * tm) % seq == 0)
    def _():
        carry_ref[...] = jnp.zeros_like(carry_ref)

    a = a_ref[...]
    g = jnp.dot(a, wg_bf_ref[...], preferred_element_type=F32)
    up = jnp.dot(a, wu_bf_ref[...], preferred_element_type=F32)
    prev = carry_ref[...]
    cw = cw_ref[...]
    conv = (cw[0:1, :] * _shift_rows(g, prev, 2) + cw[1:2, :] * _shift_rows(g, prev, 1)
            + cw[2:3, :] * g + cb_ref[...])
    t_ref[...] = (conv * jax.nn.sigmoid(conv) * up).astype(t_ref.dtype)
    carry_ref[...] = g[tm - V7X_SUBLANES:, :]


def _ffn_up(a, w_gate, w_up, layer, conv_w, conv_b, *, seq):
    n, d = a.shape
    f = w_gate.shape[2]
    tm = _row_tile(seq, 1024)
    tn = 512
    assert f % tn == 0
    pipelined = (_nbytes((tm, d), BF16) + 2 * _nbytes((d, tn), F32) + _nbytes((tm, tn), BF16)
                 + 2 * _nbytes((tm, tn), F32))
    w_spec = pl.BlockSpec((None, d, tn), lambda j, i: (layer, 0, j))
    return pl.pallas_call(
        functools.partial(_ffn_up_kernel, tm=tm, seq=seq),
        grid=(f // tn, n // tm),
        in_specs=[
            pl.BlockSpec((tm, d), lambda j, i: (i, 0)),
            w_spec,
            w_spec,
            pl.BlockSpec((CONV_WIDTH, tn), lambda j, i: (0, j)),
            pl.BlockSpec((1, tn), lambda j, i: (0, j)),
        ],
        out_specs=pl.BlockSpec((tm, tn), lambda j, i: (i, j)),
        out_shape=jax.ShapeDtypeStruct((n, f), BF16),
        scratch_shapes=[pltpu.VMEM((V7X_SUBLANES, tn), F32), pltpu.VMEM((d, tn), BF16),
                        pltpu.VMEM((d, tn), BF16)],
        compiler_params=_params(("arbitrary", "arbitrary"), pipelined, 2 * _nbytes((d, tn), BF16)),
        name="ffn_up",
    )(a, w_gate, w_up, conv_w, conv_b.reshape(1, f))


def _sc_in_kernel(a_ref, wb_ref, wc_ref, wx_ref, gb_ref, u_ref):
    a = a_ref[...]
    gb_ref[...] = jnp.dot(a, wb_ref[...], preferred_element_type=F32)
    gc = jnp.dot(a, wc_ref[...], preferred_element_type=F32)
    xi = jnp.dot(a, wx_ref[...], preferred_element_type=F32)
    u_ref[...] = gc * xi


def _sc_in(a, w_in):
    n, d = a.shape
    tm = _row_tile(n, 1024)
    tn = 512
    nj = d // tn
    pipelined = _nbytes((tm, d), BF16) + 3 * _nbytes((d, tn), BF16) + 3 * _nbytes((tm, tn), F32)
    return pl.pallas_call(
        _sc_in_kernel,
        grid=(nj, n // tm),
        in_specs=[
            pl.BlockSpec((tm, d), lambda j, i: (i, 0)),
            pl.BlockSpec((d, tn), lambda j, i: (0, j)),
            pl.BlockSpec((d, tn), lambda j, i: (0, j + nj)),
            pl.BlockSpec((d, tn), lambda j, i: (0, j + 2 * nj)),
        ],
        out_specs=[pl.BlockSpec((tm, tn), lambda j, i: (i, j))] * 2,
        out_shape=[jax.ShapeDtypeStruct((n, d), F32)] * 2,
        compiler_params=_params(("arbitrary", "arbitrary"), pipelined),
        name="sc_in",
    )(a, w_in, w_in, w_in)


def _sc_conv_kernel(u_ref, halo_ref, gb_ref, cw_ref, y_ref, *, tm, seq):
    i = pl.program_id(0)
    u = u_ref[...]
    prev = jnp.where((i * tm) % seq == 0, 0.0, halo_ref[...])
    cw = cw_ref[...]
    conv = cw[0:1, :] * _shift_rows(u, prev, 2) + cw[1:2, :] * _shift_rows(u, prev, 1) + cw[2:3, :] * u
    y_ref[...] = (gb_ref[...] * conv).astype(y_ref.dtype)


def _sc_conv(u, gb, conv_w, *, seq):
    n, d = u.shape
    tm = _row_tile(seq, 512)
    hb = tm // V7X_SUBLANES
    pipelined = 2 * _nbytes((tm, d), F32) + _nbytes((tm, d), BF16) + _nbytes((tm, d), F32)
    return pl.pallas_call(
        functools.partial(_sc_conv_kernel, tm=tm, seq=seq),
        grid=(n // tm,),
        in_specs=[
            pl.BlockSpec((tm, d), lambda i: (i, 0)),
            pl.BlockSpec((V7X_SUBLANES, d), lambda i: (jnp.maximum(i * hb - 1, 0), 0)),
            pl.BlockSpec((tm, d), lambda i: (i, 0)),
            pl.BlockSpec((CONV_WIDTH, d), lambda i: (0, 0)),
        ],
        out_specs=pl.BlockSpec((tm, d), lambda i: (i, 0)),
        out_shape=jax.ShapeDtypeStruct((n, d), BF16),
        compiler_params=_params(("arbitrary",), pipelined),
        name="sc_conv",
    )(u, u, gb, conv_w)


def _sg_in_kernel(a_ref, wu_ref, wv_ref, u_ref, v_ref):
    a = a_ref[...]
    u_ref[...] = jax.nn.gelu(jnp.dot(a, wu_ref[...], preferred_element_type=F32))
    v_ref[...] = jax.nn.gelu(jnp.dot(a, wv_ref[...], preferred_element_type=F32))


def _sg_in(a, w_in):
    n, d = a.shape
    width = w_in.shape[1] // 2
    tm = _row_tile(n, 1024)
    tn = 512
    nj = width // tn
    pipelined = _nbytes((tm, d), BF16) + 2 * _nbytes((d, tn), BF16) + 2 * _nbytes((tm, tn), F32)
    return pl.pallas_call(
        _sg_in_kernel,
        grid=(nj, n // tm),
        in_specs=[
            pl.BlockSpec((tm, d), lambda j, i: (i, 0)),
            pl.BlockSpec((d, tn), lambda j, i: (0, j)),
            pl.BlockSpec((d, tn), lambda j, i: (0, j + nj)),
        ],
        out_specs=[pl.BlockSpec((tm, tn), lambda j, i: (i, j))] * 2,
        out_shape=[jax.ShapeDtypeStruct((n, width), F32)] * 2,
        compiler_params=_params(("arbitrary", "arbitrary"), pipelined),
        name="sg_in",
    )(a, w_in, w_in)


def _sg_mix_kernel(u_ref, v_ref, lng_ref, lnb_ref, ws_ref, bs_ref, y_ref, vn_ref, *, tm):
    v = v_ref[...]
    mu = jnp.mean(v, axis=-1, keepdims=True)
    vc = v - mu
    var = jnp.mean(vc * vc, axis=-1, keepdims=True)
    vn_ref[...] = (vc * lax.rsqrt(var + EPS) * lng_ref[...] + lnb_ref[...]).astype(vn_ref.dtype)
    t = SGU_CHUNK
    nchunk = tm // t
    row = lax.broadcasted_iota(jnp.int32, (t, t), 0)
    col = lax.broadcasted_iota(jnp.int32, (t, t), 1)
    bs = bs_ref[...]
    for g in range(SGU_GROUPS):
        cols = slice(g * t, (g + 1) * t)
        w = jnp.where(row >= col, ws_ref[g], 0.0).astype(BF16)
        rhs = jnp.concatenate([vn_ref[c * t:(c + 1) * t, cols] for c in range(nchunk)], axis=1)
        mixed = jnp.dot(w, rhs, preferred_element_type=F32) + bs[:, g:g + 1]
        for c in range(nchunk):
            rows = slice(c * t, (c + 1) * t)
            y_ref[rows, cols] = (u_ref[rows, cols] * mixed[:, c * t:(c + 1) * t]).astype(y_ref.dtype)


def _sg_mix(u, v, ln_g, ln_b, w_s, b_s):
    n, width = u.shape
    tm = _row_tile(n, 512)
    assert tm % SGU_CHUNK == 0
    pipelined = 2 * _nbytes((tm, width), F32) + _nbytes((tm, width), BF16)
    resident = 2 * _nbytes(w_s.shape, F32) + _nbytes((tm, width), BF16)
    return pl.pallas_call(
        functools.partial(_sg_mix_kernel, tm=tm),
        grid=(n // tm,),
        in_specs=[
            pl.BlockSpec((tm, width), lambda i: (i, 0)),
            pl.BlockSpec((tm, width), lambda i: (i, 0)),
            pl.BlockSpec((1, width), lambda i: (0, 0)),
            pl.BlockSpec((1, width), lambda i: (0, 0)),
            pl.BlockSpec(w_s.shape, lambda i: (0, 0, 0)),
            pl.BlockSpec((SGU_CHUNK, SGU_GROUPS), lambda i: (0, 0)),
        ],
        out_specs=pl.BlockSpec((tm, width), lambda i: (i, 0)),
        out_shape=jax.ShapeDtypeStruct((n, width), BF16),
        scratch_shapes=[pltpu.VMEM((tm, width), BF16)],
        compiler_params=_params(("arbitrary",), pipelined, resident),
        name="sg_mix",
    )(u, v, ln_g.reshape(1, width), ln_b.reshape(1, width), w_s, b_s.T)


DIFF_TQ = 512
DIFF_TK = 512
DIFF_HEADS_PER_STEP = 2
SB_TQ = 256
SB_TK = 256
SB_HEADS_PER_STEP = 4
SB_DEAD_LOG2 = -160.0
LOG2E = 1.4426950408889634


def _rope_table_kernel(pos_ref, invf_ref, sign_ref, cos_ref, sin_ref):
    ang = pos_ref[...] * invf_ref[...]
    cos_ref[...] = jnp.cos(ang)
    sin_ref[...] = jnp.sin(ang) * sign_ref[...]


def _rope_tables(positions, head_dim):
    n = positions.size
    half = head_dim // 2
    inv_freq = 1.0 / (ROPE_THETA ** (jnp.arange(0, head_dim, 2, dtype=F32) / head_dim))
    lane = jnp.arange(V7X_LANES)
    invf = inv_freq[lane % half].reshape(1, V7X_LANES)
    sign = jnp.where(lane % head_dim < half, -1.0, 1.0).astype(F32).reshape(1, V7X_LANES)
    tm = _row_tile(n, 1024)
    return pl.pallas_call(
        _rope_table_kernel,
        grid=(n // tm,),
        in_specs=[
            pl.BlockSpec((tm, 1), lambda i: (i, 0)),
            pl.BlockSpec((1, V7X_LANES), lambda i: (0, 0)),
            pl.BlockSpec((1, V7X_LANES), lambda i: (0, 0)),
        ],
        out_specs=[pl.BlockSpec((tm, V7X_LANES), lambda i: (i, 0))] * 2,
        out_shape=[jax.ShapeDtypeStruct((n, V7X_LANES), F32)] * 2,
        compiler_params=_params(("arbitrary",), 3 * _nbytes((tm, V7X_LANES), F32)),
        name="rope_tables",
    )(positions.astype(F32).reshape(n, 1), invf, sign)


def _rotate_half(x, cos, sin, first_half):
    partner = jnp.where(first_half, pltpu.roll(x, 96, 1), pltpu.roll(x, 32, 1))
    return x * cos + partner * sin


def _qkv_kernel(a_ref, wq_ref, wk_ref, wv_ref, *refs, rope, q_scale, tk):
    if rope:
        cos_ref, sin_ref, q_ref, k_ref, vt_ref = refs
    else:
        q_ref, k_ref, vt_ref = refs
    a = a_ref[...]
    q = jnp.dot(a, wq_ref[...], preferred_element_type=F32)
    k = jnp.dot(a, wk_ref[...], preferred_element_type=F32)
    v = jnp.dot(a, wv_ref[...], preferred_element_type=F32)
    tm, tn = q.shape
    if rope:
        cos = cos_ref[...]
        sin = sin_ref[...]
        lane = lax.broadcasted_iota(jnp.int32, cos.shape, 1)
        first_half = (lane % 64) < 32
        for c in range(tn // V7X_LANES):
            cols = slice(c * V7X_LANES, (c + 1) * V7X_LANES)
            q_ref[:, cols] = (_rotate_half(q[:, cols], cos, sin, first_half) * q_scale).astype(q_ref.dtype)
            k_ref[:, cols] = _rotate_half(k[:, cols], cos, sin, first_half).astype(k_ref.dtype)
    else:
        q_ref[...] = q.astype(q_ref.dtype)
        k_ref[...] = k.astype(k_ref.dtype)
    for hh in range(tn // HEAD_WIDTH):
        for kk in range(tm // tk):
            blk = v[kk * tk:(kk + 1) * tk, hh * HEAD_WIDTH:(hh + 1) * HEAD_WIDTH]
            vt_ref[hh, kk] = blk.T.astype(vt_ref.dtype)


def _qkv(a, w_qkv, *, tk, rope_tables=None, q_scale=1.0):
    n, d = a.shape
    width = w_qkv.shape[1] // 3
    tm = _row_tile(n, 1024)
    tn = 512
    nj = width // tn
    rope = rope_tables is not None
    in_specs = [
        pl.BlockSpec((tm, d), lambda j, i: (i, 0)),
        pl.BlockSpec((d, tn), lambda j, i: (0, j)),
        pl.BlockSpec((d, tn), lambda j, i: (0, j + nj)),
        pl.BlockSpec((d, tn), lambda j, i: (0, j + 2 * nj)),
    ]
    args = [a, w_qkv, w_qkv, w_qkv]
    if rope:
        in_specs += [pl.BlockSpec((tm, V7X_LANES), lambda j, i: (i, 0))] * 2
        args += list(rope_tables)
    heads = width // HEAD_WIDTH
    pipelined = (_nbytes((tm, d), BF16) + 3 * _nbytes((d, tn), BF16) + 3 * _nbytes((tm, tn), BF16)
                 + 3 * _nbytes((tm, tn), F32) + 2 * _nbytes((tm, V7X_LANES), F32))
    return pl.pallas_call(
        functools.partial(_qkv_kernel, rope=rope, q_scale=q_scale, tk=tk),
        grid=(nj, n // tm),
        in_specs=in_specs,
        out_specs=[
            pl.BlockSpec((tm, tn), lambda j, i: (i, j)),
            pl.BlockSpec((tm, tn), lambda j, i: (i, j)),
            pl.BlockSpec((tn // HEAD_WIDTH, tm // tk, HEAD_WIDTH, tk), lambda j, i: (j, i, 0, 0)),
        ],
        out_shape=[
            jax.ShapeDtypeStruct((n, width), BF16),
            jax.ShapeDtypeStruct((n, width), BF16),
            jax.ShapeDtypeStruct((heads, n // tk, HEAD_WIDTH, tk), BF16),
        ],
        compiler_params=_params(("arbitrary", "arbitrary"), pipelined),
        name="qkv_rope" if rope else "qkv",
    )(*args)


def _attn_specs(seq, tq, tk, heads):
    nq = seq // tq
    nk = seq // tk
    w = heads * HEAD_WIDTH
    q_spec = pl.BlockSpec((tq, w), lambda b, h, qi: (b * nq + qi, h))
    k_spec = pl.BlockSpec((seq, w), lambda b, h, qi: (b, h))
    vt_spec = pl.BlockSpec((heads, nk, HEAD_WIDTH, tk), lambda b, h, qi: (h, b, 0, 0))
    o_spec = pl.BlockSpec((tq, w), lambda b, h, qi: (b * nq + qi, h))
    return q_spec, k_spec, vt_spec, o_spec


def _attn_params(seq, tq, tk, heads, score_cols):
    w = heads * HEAD_WIDTH
    pipelined = 2 * _nbytes((tq, w), BF16) + 2 * _nbytes((seq, w), BF16)
    scratch = (_nbytes((2, heads, tk, score_cols), F32) + _nbytes((2, heads, tk, score_cols), BF16)
               + _nbytes((heads, HEAD_WIDTH, score_cols), F32))
    return _params(("arbitrary", "arbitrary", "arbitrary"), pipelined, scratch)


def _head_cols(hh):
    return slice(hh * HEAD_WIDTH, (hh + 1) * HEAD_WIDTH)


def _diff_attn_kernel(q_ref, k_ref, vt_ref, lq1_ref, lk1_ref, lq2_ref, lk2_ref, g_ref, o_ref,
                      qq_ref, s_ref, p_ref, alpha_ref, m_ref, l_ref, acc_ref, *,
                      tq, tk, heads, lambda_init):
    qi = pl.program_id(2)
    lane = lax.broadcasted_iota(jnp.int32, (tq, HEAD_WIDTH), 1)
    p_ref[...] = jnp.zeros_like(p_ref)
    alpha_ref[...] = jnp.ones_like(alpha_ref)
    for hh in range(heads):
        q = q_ref[:, _head_cols(hh)]
        zero = jnp.zeros_like(q)
        qq_ref[hh, :tq, :] = jnp.where(lane < 64, q, zero)
        qq_ref[hh, tq:, :] = jnp.where(lane >= 64, q, zero)
        m_ref[hh] = jnp.full(m_ref.shape[1:], -jnp.inf, F32)
        l_ref[hh] = jnp.zeros(l_ref.shape[1:], F32)
        acc_ref[hh] = jnp.zeros(acc_ref.shape[1:], F32)

    chunks = [(hh, slice(c * tq, (c + 1) * tq)) for hh in range(heads) for c in range(2)]

    def scores_to(slot, kb, hh, cols):
        rows = pl.ds(pl.multiple_of(kb * tk, tk), tk)
        s_ref[slot, hh, :, cols] = lax.dot_general(
            k_ref[rows, _head_cols(hh)], qq_ref[hh, cols, :], (((1,), (1,)), ((), ())),
            preferred_element_type=F32)

    def accumulate(slot, kb, hh, cols):
        acc_ref[hh, :, cols] = alpha_ref[slot, hh, :, cols] * acc_ref[hh, :, cols] + jnp.dot(
            vt_ref[hh, kb], p_ref[slot, hh, :, cols], preferred_element_type=F32)

    def softmax_to(slot, kb, hh, cols, diagonal):
        s = s_ref[slot, hh, :, cols]
        if diagonal:
            key = kb * tk + lax.broadcasted_iota(jnp.int32, s.shape, 0)
            qry = qi * tq + lax.broadcasted_iota(jnp.int32, s.shape, 1)
            s = jnp.where(key <= qry, s, -jnp.inf)
        m = m_ref[hh, :, cols]
        m_new = jnp.maximum(m, jnp.max(s, axis=0, keepdims=True))
        alpha = jnp.exp2(m - m_new)
        p = jnp.exp2(s - m_new)
        m_ref[hh, :, cols] = m_new
        l_ref[hh, :, cols] = alpha * l_ref[hh, :, cols] + jnp.sum(p, axis=0, keepdims=True)
        alpha_ref[slot, hh, :, cols] = alpha
        p_ref[slot, hh, :, cols] = p.astype(BF16)

    def step(slot, kb, diagonal):
        other = 1 - slot
        for hh, cols in chunks:
            accumulate(other, jnp.maximum(kb - 1, 0), hh, cols)
            if not diagonal:
                scores_to(other, kb + 1, hh, cols)
            softmax_to(slot, kb, hh, cols, diagonal)

    nfull = (qi * tq) // tk
    @pl.when(nfull % 2 == 0)
    def _():
        for hh, cols in chunks:
            scores_to(0, 0, hh, cols)

    @pl.when(nfull % 2 == 1)
    def _():
        for hh, cols in chunks:
            scores_to(1, 0, hh, cols)

    def body(kb, _):
        @pl.when((nfull - kb) % 2 == 0)
        def _():
            step(0, kb, False)

        @pl.when((nfull - kb) % 2 == 1)
        def _():
            step(1, kb, False)

        return 0

    lax.fori_loop(0, nfull, body, 0)
    step(0, nfull, True)
    for hh, cols in chunks:
        accumulate(0, nfull, hh, cols)

    lam = (jnp.exp(jnp.sum(lq1_ref[...] * lk1_ref[...], axis=-1, keepdims=True))
           - jnp.exp(jnp.sum(lq2_ref[...] * lk2_ref[...], axis=-1, keepdims=True)) + lambda_init)
    for hh in range(heads):
        o = acc_ref[hh] / l_ref[hh]
        o = o[:, :tq] - lam * o[:, tq:]
        ms = jnp.mean(o * o, axis=0, keepdims=True)
        o = o * lax.rsqrt(ms + EPS) * g_ref[...] * (1.0 - lambda_init)
        o_ref[:, _head_cols(hh)] = o.T.astype(o_ref.dtype)


def _diff_attn(q, k, vt, lq1, lk1, lq2, lk2, subln_g, *, batch, seq, lambda_init):
    n, width = q.shape
    tq, tk, heads = DIFF_TQ, DIFF_TK, DIFF_HEADS_PER_STEP
    q_spec, k_spec, vt_spec, o_spec = _attn_specs(seq, tq, tk, heads)
    dd = lq1.shape[0]
    small = pl.BlockSpec((1, dd), lambda b, h, qi: (0, 0))
    return pl.pallas_call(
        functools.partial(_diff_attn_kernel, tq=tq, tk=tk, heads=heads, lambda_init=lambda_init),
        grid=(batch, width // (heads * HEAD_WIDTH), seq // tq),
        in_specs=[q_spec, k_spec, vt_spec, small, small, small, small,
                  pl.BlockSpec((HEAD_WIDTH, 1), lambda b, h, qi: (0, 0))],
        out_specs=o_spec,
        out_shape=jax.ShapeDtypeStruct((n, width), BF16),
        scratch_shapes=[
            pltpu.VMEM((heads, 2 * tq, HEAD_WIDTH), BF16),
            pltpu.VMEM((2, heads, tk, 2 * tq), F32),
            pltpu.VMEM((2, heads, tk, 2 * tq), BF16),
            pltpu.VMEM((2, heads, 1, 2 * tq), F32),
            pltpu.VMEM((heads, 1, 2 * tq), F32),
            pltpu.VMEM((heads, 1, 2 * tq), F32),
            pltpu.VMEM((heads, HEAD_WIDTH, 2 * tq), F32),
        ],
        compiler_params=_attn_params(seq, tq, tk, heads, 2 * tq),
        name="diff_attn",
    )(q, k, vt, lq1.reshape(1, dd), lk1.reshape(1, dd), lq2.reshape(1, dd), lk2.reshape(1, dd),
      subln_g.reshape(HEAD_WIDTH, 1))


def _sb_attn_kernel(q_ref, k_ref, vt_ref, o_ref, later_ref, z_ref, tail_ref, acc_ref, *,
                    tq, tk, heads, scale):
    qi = pl.program_id(2)
    row = lax.broadcasted_iota(jnp.int32, (tk, tk), 0)
    col = lax.broadcasted_iota(jnp.int32, (tk, tk), 1)
    later_ref[...] = jnp.where(col > row, -1.0, 0.0).astype(BF16)
    tail_ref[...] = jnp.zeros_like(tail_ref)
    acc_ref[...] = jnp.zeros_like(acc_ref)

    def scores_to(slot, kb):
        rows = pl.ds(pl.multiple_of(kb * tk, tk), tk)
        for hh in range(heads):
            z_ref[slot, hh] = lax.dot_general(
                k_ref[rows, _head_cols(hh)], q_ref[:, _head_cols(hh)], (((1,), (1,)), ((), ())),
                preferred_element_type=F32) * (scale * LOG2E)

    def block(slot, kb, diagonal):
        if diagonal:
            key = kb * tk + lax.broadcasted_iota(jnp.int32, (tk, tq), 0)
            qry = qi * tq + lax.broadcasted_iota(jnp.int32, (tk, tq), 1)
            valid = key < qry
        neg_later = later_ref[...]
        first_rows, his, los = [], [], []
        for hh in range(heads):
            z = z_ref[slot, hh]
            softplus = jnp.maximum(z, 0.0) + jnp.log2(1.0 + jnp.exp2(-jnp.abs(z)))
            z_ref[slot, hh] = z - softplus
            if diagonal:
                softplus = jnp.where(valid, softplus, 0.0)
            hi = softplus.astype(BF16)
            first_rows.append(softplus[0:1, :])
            his.append(hi)
            los.append((softplus - hi.astype(F32)).astype(BF16))
        survives = [jnp.dot(neg_later, his[hh], preferred_element_type=F32)
                    + jnp.dot(neg_later, los[hh], preferred_element_type=F32) for hh in range(heads)]
        weights = []
        for hh in range(heads):
            tail = tail_ref[hh]
            a = jnp.exp2(z_ref[slot, hh] + survives[hh] + tail)
            if diagonal:
                a = jnp.where(valid, a, 0.0)
            weights.append(a.astype(BF16))
            tail_ref[hh] = tail + survives[hh][0:1, :] - first_rows[hh]
        for hh in range(heads):
            acc_ref[hh] += jnp.dot(vt_ref[hh, kb], weights[hh], preferred_element_type=F32)

    scores_to(0, qi)
    scores_to(1, jnp.maximum(qi - 1, 0))
    block(0, qi, True)

    def any_live():
        return jnp.max(tail_ref[...]) > SB_DEAD_LOG2

    def body(c):
        t = c[0]
        kb = qi - 1 - t

        @pl.when(t % 2 == 0)
        def _():
            scores_to(0, jnp.maximum(kb - 1, 0))
            block(1, kb, False)

        @pl.when(t % 2 == 1)
        def _():
            scores_to(1, jnp.maximum(kb - 1, 0))
            block(0, kb, False)

        return t + 1, any_live()

    lax.while_loop(lambda c: jnp.logical_and(c[0] < qi, c[1]), body, (jnp.int32(0), any_live()))
    for hh in range(heads):
        o_ref[:, _head_cols(hh)] = acc_ref[hh].T.astype(o_ref.dtype)


def _sb_attn(q, k, vt, *, batch, seq, scale):
    n, width = q.shape
    tq, tk, heads = SB_TQ, SB_TK, SB_HEADS_PER_STEP
    assert tq == tk
    q_spec, k_spec, vt_spec, o_spec = _attn_specs(seq, tq, tk, heads)
    return pl.pallas_call(
        functools.partial(_sb_attn_kernel, tq=tq, tk=tk, heads=heads, scale=scale),
        grid=(batch, width // (heads * HEAD_WIDTH), seq // tq),
        in_specs=[q_spec, k_spec, vt_spec],
        out_specs=o_spec,
        out_shape=jax.ShapeDtypeStruct((n, width), BF16),
        scratch_shapes=[
            pltpu.VMEM((tk, tk), BF16),
            pltpu.VMEM((2, heads, tk, tq), F32),
            pltpu.VMEM((heads, 1, tq), F32),
            pltpu.VMEM((heads, HEAD_WIDTH, tq), F32),
        ],
        compiler_params=_attn_params(seq, tq, tk, heads, tq),
        name="sb_attn",
    )(q, k, vt)


def kernel(x, positions, norm_mix_g, norm_ffn_g, norm_final_g, sc_w_in, sc_conv_w, sc_w_out, sg_w_in, sg_ln_g, sg_ln_b, sg_w_s, sg_b_s, sg_w_out, da_w_qkv, da_lambda_q1, da_lambda_k1, da_lambda_q2, da_lambda_k2, da_subln_g, da_w_out, sb_w_qkv, sb_w_out, ffn_w_gate, ffn_w_up, ffn_conv_w, ffn_conv_b, ffn_w_down):
    batch, seq, d = x.shape
    depth = norm_mix_g.shape[0]
    n = batch * seq
    assert seq % DIFF_TK == 0 and seq % DIFF_TQ == 0 and seq % SB_TQ == 0 and d % HEAD_WIDTH == 0

    def bf(w):
        return w.astype(BF16)

    h = x.reshape(n, d)
    a = _rmsnorm(h, norm_mix_g[0])
    for layer in range(depth):
        mixer, j = layer % 4, layer // 4
        if mixer == 0:
            gb, u = _sc_in(a, bf(sc_w_in[j]))
            y = _sc_conv(u, gb, sc_conv_w[j], seq=seq)
            w_out = sc_w_out[j]
        elif mixer == 1:
            u, v = _sg_in(a, bf(sg_w_in[j]))
            y = _sg_mix(u, v, sg_ln_g[j], sg_ln_b[j], sg_w_s[j], sg_b_s[j])
            w_out = sg_w_out[j]
        elif mixer == 2:
            head_dim = d // DIFF_HEADS // 2
            lambda_init = 0.8 - 0.6 * math.exp(-0.3 * layer)
            tables = _rope_tables(positions, head_dim)
            q, k, vt = _qkv(a, bf(da_w_qkv[j]), tk=DIFF_TK, rope_tables=tables,
                            q_scale=head_dim ** -0.5 * LOG2E)
            y = _diff_attn(q, k, vt, da_lambda_q1[j], da_lambda_k1[j], da_lambda_q2[j], da_lambda_k2[j],
                           da_subln_g[j], batch=batch, seq=seq, lambda_init=lambda_init)
            w_out = da_w_out[j]
        else:
            q, k, vt = _qkv(a, bf(sb_w_qkv[j]), tk=SB_TK)
            y = _sb_attn(q, k, vt, batch=batch, seq=seq, scale=(d // SB_HEADS) ** -0.5)
            w_out = sb_w_out[j]
        h, a = _out_proj(y, bf(w_out), h, norm_ffn_g[layer])
        t = _ffn_up(a, ffn_w_gate, ffn_w_up, layer, ffn_conv_w[layer], ffn_conv_b[layer], seq=seq)
        last = layer == depth - 1
        h, a = _out_proj(t, bf(ffn_w_down[layer]), h, norm_final_g if last else norm_mix_g[layer + 1], final=last)
    return a.reshape(batch, seq, d)
```

```python
import functools
import math

import jax
import jax.numpy as jnp
from jax import lax
from jax.experimental import pallas as pl
from jax.experimental.pallas import tpu as pltpu

F32 = jnp.float32
BF16 = jnp.bfloat16

EPS = 1e-6
ROPE_THETA = 10000.0
CONV_WIDTH = 3
SGU_GROUPS = 16
SGU_CHUNK = 128
DIFF_HEADS = 16
SB_HEADS = 16
HEAD_WIDTH = 128

V7X_LANES = 128
V7X_SUBLANES = 8
V7X_VMEM_BYTES = 64 * 1024 * 1024
INTERNAL_SCRATCH_BYTES = 12 * 1024 * 1024


def _nbytes(shape, dtype):
    return math.prod(shape) * jnp.dtype(dtype).itemsize


def _params(semantics, pipelined_bytes, resident_bytes=0):
    limit = 2 * pipelined_bytes + resident_bytes + INTERNAL_SCRATCH_BYTES
    limit = min(limit, V7X_VMEM_BYTES - 4 * 1024 * 1024)
    return pltpu.CompilerParams(dimension_semantics=semantics, vmem_limit_bytes=int(limit))


def _row_tile(n, want):
    t = min(n, want)
    assert n % t == 0, (n, t)
    return t


def _rms(x, g):
    ms = jnp.mean(x * x, axis=-1, keepdims=True)
    return x * lax.rsqrt(ms + EPS) * g


def _shift_rows(x, prev, k):
    row = lax.broadcasted_iota(jnp.int32, x.shape, 0)
    out = pltpu.roll(x, k, 0)
    for r in range(k):
        out = jnp.where(row == r, prev[V7X_SUBLANES - k + r:V7X_SUBLANES - k + r + 1, :], out)
    return out


def _rmsnorm_kernel(x_ref, g_ref, o_ref):
    o_ref[...] = _rms(x_ref[...], g_ref[...]).astype(o_ref.dtype)


def _rmsnorm(h, g):
    n, d = h.shape
    tm = _row_tile(n, 512)
    return pl.pallas_call(
        _rmsnorm_kernel,
        grid=(n // tm,),
        in_specs=[pl.BlockSpec((tm, d), lambda i: (i, 0)), pl.BlockSpec((1, d), lambda i: (0, 0))],
        out_specs=pl.BlockSpec((tm, d), lambda i: (i, 0)),
        out_shape=jax.ShapeDtypeStruct((n, d), BF16),
        compiler_params=_params(("arbitrary",), _nbytes((tm, d), F32) + _nbytes((tm, d), BF16)),
        name="rmsnorm",
    )(h, g.reshape(1, d))


def _out_proj_kernel(x_ref, w_ref, h_ref, g_ref, *out_refs, emit_h):
    hn = h_ref[...] + jnp.dot(x_ref[...], w_ref[...], preferred_element_type=F32)
    if emit_h:
        out_refs[0][...] = hn
    an_ref = out_refs[-1]
    an_ref[...] = _rms(hn, g_ref[...]).astype(an_ref.dtype)


def _out_proj(x, w, h, g, *, final=False):
    n, k = x.shape
    d = w.shape[1]
    tm = _row_tile(n, 256)
    an_dtype = F32 if final else BF16
    out_shape = [jax.ShapeDtypeStruct((n, d), an_dtype)]
    out_specs = [pl.BlockSpec((tm, d), lambda i: (i, 0))]
    if not final:
        out_shape.insert(0, jax.ShapeDtypeStruct((n, d), F32))
        out_specs.insert(0, pl.BlockSpec((tm, d), lambda i: (i, 0)))
    pipelined = (_nbytes((tm, k), BF16) + 2 * _nbytes((tm, d), F32) + _nbytes((tm, d), an_dtype))
    outs = pl.pallas_call(
        functools.partial(_out_proj_kernel, emit_h=not final),
        grid=(n // tm,),
        in_specs=[
            pl.BlockSpec((tm, k), lambda i: (i, 0)),
            pl.BlockSpec((k, d), lambda i: (0, 0), pipeline_mode=pl.Buffered(1)),
            pl.BlockSpec((tm, d), lambda i: (i, 0)),
            pl.BlockSpec((1, d), lambda i: (0, 0)),
        ],
        out_specs=out_specs,
        out_shape=out_shape,
        compiler_params=_params(("arbitrary",), pipelined, _nbytes((k, d), BF16)),
        name="out_proj",
    )(x, w, h, g.reshape(1, d))
    return (None, outs[0]) if final else (outs[0], outs[1])


def _ffn_up_kernel(a_ref, wg_ref, wu_ref, cw_ref, cb_ref, t_ref, carry_ref, wg_bf_ref, wu_bf_ref, *,
                   tm, seq):
    i = pl.program_id(1)

    @pl.when(i == 0)
    def _():
        wg_bf_ref[...] = wg_ref[...].astype(BF16)
        wu_bf_ref[...] = wu_ref[...].astype(BF16)

    @pl.when((i * tm) % seq == 0)
    def _():
        carry_ref[...] = jnp.zeros_like(carry_ref)

    a = a_ref[...]
    g = jnp.dot(a, wg_bf_ref[...], preferred_element_type=F32)
    up = jnp.dot(a, wu_bf_ref[...], preferred_element_type=F32)
    prev = carry_ref[...]
    cw = cw_ref[...]
    conv = (cw[0:1, :] * _shift_rows(g, prev, 2) + cw[1:2, :] * _shift_rows(g, prev, 1)
            + cw[2:3, :] * g + cb_ref[...])
    t_ref[...] = (conv * jax.nn.sigmoid(conv) * up).astype(t_ref.dtype)
    carry_ref[...] = g[tm - V7X_SUBLANES:, :]


def _ffn_up(a, w_gate, w_up, layer, conv_w, conv_b, *, seq):
    n, d = a.shape
    f = w_gate.shape[2]
    tm = _row_tile(seq, 1024)
    tn = 512
    assert f % tn == 0
    pipelined = (_nbytes((tm, d), BF16) + 2 * _nbytes((d, tn), F32) + _nbytes((tm, tn), BF16)
                 + 2 * _nbytes((tm, tn), F32))
    w_spec = pl.BlockSpec((None, d, tn), lambda j, i: (layer, 0, j))
    return pl.pallas_call(
        functools.partial(_ffn_up_kernel, tm=tm, seq=seq),
        grid=(f // tn, n // tm),
        in_specs=[
            pl.BlockSpec((tm, d), lambda j, i: (i, 0)),
            w_spec,
            w_spec,
            pl.BlockSpec((CONV_WIDTH, tn), lambda j, i: (0, j)),
            pl.BlockSpec((1, tn), lambda j, i: (0, j)),
        ],
        out_specs=pl.BlockSpec((tm, tn), lambda j, i: (i, j)),
        out_shape=jax.ShapeDtypeStruct((n, f), BF16),
        scratch_shapes=[pltpu.VMEM((V7X_SUBLANES, tn), F32), pltpu.VMEM((d, tn), BF16),
                        pltpu.VMEM((d, tn), BF16)],
        compiler_params=_params(("arbitrary", "arbitrary"), pipelined, 2 * _nbytes((d, tn), BF16)),
        name="ffn_up",
    )(a, w_gate, w_up, conv_w, conv_b.reshape(1, f))


def _sc_in_kernel(a_ref, wb_ref, wc_ref, wx_ref, gb_ref, u_ref):
    a = a_ref[...]
    gb_ref[...] = jnp.dot(a, wb_ref[...], preferred_element_type=F32)
    gc = jnp.dot(a, wc_ref[...], preferred_element_type=F32)
    xi = jnp.dot(a, wx_ref[...], preferred_element_type=F32)
    u_ref[...] = gc * xi


def _sc_in(a, w_in):
    n, d = a.shape
    tm = _row_tile(n, 1024)
    tn = 512
    nj = d // tn
    pipelined = _nbytes((tm, d), BF16) + 3 * _nbytes((d, tn), BF16) + 3 * _nbytes((tm, tn), F32)
    return pl.pallas_call(
        _sc_in_kernel,
        grid=(nj, n // tm),
        in_specs=[
            pl.BlockSpec((tm, d), lambda j, i: (i, 0)),
            pl.BlockSpec((d, tn), lambda j, i: (0, j)),
            pl.BlockSpec((d, tn), lambda j, i: (0, j + nj)),
            pl.BlockSpec((d, tn), lambda j, i: (0, j + 2 * nj)),
        ],
        out_specs=[pl.BlockSpec((tm, tn), lambda j, i: (i, j))] * 2,
        out_shape=[jax.ShapeDtypeStruct((n, d), F32)] * 2,
        compiler_params=_params(("arbitrary", "arbitrary"), pipelined),
        name="sc_in",
    )(a, w_in, w_in, w_in)


def _sc_conv_kernel(u_ref, halo_ref, gb_ref, cw_ref, y_ref, *, tm, seq):
    i = pl.program_id(0)
    u = u_ref[...]
    prev = jnp.where((i * tm) % seq == 0, 0.0, halo_ref[...])
    cw = cw_ref[...]
    conv = cw[0:1, :] * _shift_rows(u, prev, 2) + cw[1:2, :] * _shift_rows(u, prev, 1) + cw[2:3, :] * u
    y_ref[...] = (gb_ref[...] * conv).astype(y_ref.dtype)


def _sc_conv(u, gb, conv_w, *, seq):
    n, d = u.shape
    tm = _row_tile(seq, 512)
    hb = tm // V7X_SUBLANES
    pipelined = 2 * _nbytes((tm, d), F32) + _nbytes((tm, d), BF16) + _nbytes((tm, d), F32)
    return pl.pallas_call(
        functools.partial(_sc_conv_kernel, tm=tm, seq=seq),
        grid=(n // tm,),
        in_specs=[
            pl.BlockSpec((tm, d), lambda i: (i, 0)),
            pl.BlockSpec((V7X_SUBLANES, d), lambda i: (jnp.maximum(i * hb - 1, 0), 0)),
            pl.BlockSpec((tm, d), lambda i: (i, 0)),
            pl.BlockSpec((CONV_WIDTH, d), lambda i: (0, 0)),
        ],
        out_specs=pl.BlockSpec((tm, d), lambda i: (i, 0)),
        out_shape=jax.ShapeDtypeStruct((n, d), BF16),
        compiler_params=_params(("arbitrary",), pipelined),
        name="sc_conv",
    )(u, u, gb, conv_w)


def _sg_in_kernel(a_ref, wu_ref, wv_ref, u_ref, v_ref):
    a = a_ref[...]
    u_ref[...] = jax.nn.gelu(jnp.dot(a, wu_ref[...], preferred_element_type=F32))
    v_ref[...] = jax.nn.gelu(jnp.dot(a, wv_ref[...], preferred_element_type=F32))


def _sg_in(a, w_in):
    n, d = a.shape
    width = w_in.shape[1] // 2
    tm = _row_tile(n, 1024)
    tn = 512
    nj = width // tn
    pipelined = _nbytes((tm, d), BF16) + 2 * _nbytes((d, tn), BF16) + 2 * _nbytes((tm, tn), F32)
    return pl.pallas_call(
        _sg_in_kernel,
        grid=(nj, n // tm),
        in_specs=[
            pl.BlockSpec((tm, d), lambda j, i: (i, 0)),
            pl.BlockSpec((d, tn), lambda j, i: (0, j)),
            pl.BlockSpec((d, tn), lambda j, i: (0, j + nj)),
        ],
        out_specs=[pl.BlockSpec((tm, tn), lambda j, i: (i, j))] * 2,
        out_shape=[jax.ShapeDtypeStruct((n, width), F32)] * 2,
        compiler_params=_params(("arbitrary", "arbitrary"), pipelined),
        name="sg_in",
    )(a, w_in, w_in)


def _sg_mix_kernel(u_ref, v_ref, lng_ref, lnb_ref, ws_ref, bs_ref, y_ref, vn_ref, *, tm):
    v = v_ref[...]
    mu = jnp.mean(v, axis=-1, keepdims=True)
    vc = v - mu
    var = jnp.mean(vc * vc, axis=-1, keepdims=True)
    vn_ref[...] = (vc * lax.rsqrt(var + EPS) * lng_ref[...] + lnb_ref[...]).astype(vn_ref.dtype)
    t = SGU_CHUNK
    nchunk = tm // t
    row = lax.broadcasted_iota(jnp.int32, (t, t), 0)
    col = lax.broadcasted_iota(jnp.int32, (t, t), 1)
    bs = bs_ref[...]
    for g in range(SGU_GROUPS):
        cols = slice(g * t, (g + 1) * t)
        w = jnp.where(row >= col, ws_ref[g], 0.0).astype(BF16)
        rhs = jnp.concatenate([vn_ref[c * t:(c + 1) * t, cols] for c in range(nchunk)], axis=1)
        mixed = jnp.dot(w, rhs, preferred_element_type=F32) + bs[:, g:g + 1]
        for c in range(nchunk):
            rows = slice(c * t, (c + 1) * t)
            y_ref[rows, cols] = (u_ref[rows, cols] * mixed[:, c * t:(c + 1) * t]).astype(y_ref.dtype)


def _sg_mix(u, v, ln_g, ln_b, w_s, b_s):
    n, width = u.shape
    tm = _row_tile(n, 512)
    assert tm % SGU_CHUNK == 0
    pipelined = 2 * _nbytes((tm, width), F32) + _nbytes((tm, width), BF16)
    resident = 2 * _nbytes(w_s.shape, F32) + _nbytes((tm, width), BF16)
    return pl.pallas_call(
        functools.partial(_sg_mix_kernel, tm=tm),
        grid=(n // tm,),
        in_specs=[
            pl.BlockSpec((tm, width), lambda i: (i, 0)),
            pl.BlockSpec((tm, width), lambda i: (i, 0)),
            pl.BlockSpec((1, width), lambda i: (0, 0)),
            pl.BlockSpec((1, width), lambda i: (0, 0)),
            pl.BlockSpec(w_s.shape, lambda i: (0, 0, 0)),
            pl.BlockSpec((SGU_CHUNK, SGU_GROUPS), lambda i: (0, 0)),
        ],
        out_specs=pl.BlockSpec((tm, width), lambda i: (i, 0)),
        out_shape=jax.ShapeDtypeStruct((n, width), BF16),
        scratch_shapes=[pltpu.VMEM((tm, width), BF16)],
        compiler_params=_params(("arbitrary",), pipelined, resident),
        name="sg_mix",
    )(u, v, ln_g.reshape(1, width), ln_b.reshape(1, width), w_s, b_s.T)


DIFF_TQ = 512
DIFF_TK = 512
DIFF_HEADS_PER_STEP = 2
DIFF_CHUNK = 256
SB_TQ = 256
SB_TK = 256
SB_HEADS_PER_STEP = 4
SB_DEAD_LOG2 = -160.0
LOG2E = 1.4426950408889634


def _rope_table_kernel(pos_ref, invf_ref, sign_ref, cos_ref, sin_ref):
    ang = pos_ref[...] * invf_ref[...]
    cos_ref[...] = jnp.cos(ang)
    sin_ref[...] = jnp.sin(ang) * sign_ref[...]


def _rope_tables(positions, head_dim):
    n = positions.size
    half = head_dim // 2
    inv_freq = 1.0 / (ROPE_THETA ** (jnp.arange(0, head_dim, 2, dtype=F32) / head_dim))
    lane = jnp.arange(V7X_LANES)
    invf = inv_freq[lane % half].reshape(1, V7X_LANES)
    sign = jnp.where(lane % head_dim < half, -1.0, 1.0).astype(F32).reshape(1, V7X_LANES)
    tm = _row_tile(n, 1024)
    return pl.pallas_call(
        _rope_table_kernel,
        grid=(n // tm,),
        in_specs=[
            pl.BlockSpec((tm, 1), lambda i: (i, 0)),
            pl.BlockSpec((1, V7X_LANES), lambda i: (0, 0)),
            pl.BlockSpec((1, V7X_LANES), lambda i: (0, 0)),
        ],
        out_specs=[pl.BlockSpec((tm, V7X_LANES), lambda i: (i, 0))] * 2,
        out_shape=[jax.ShapeDtypeStruct((n, V7X_LANES), F32)] * 2,
        compiler_params=_params(("arbitrary",), 3 * _nbytes((tm, V7X_LANES), F32)),
        name="rope_tables",
    )(positions.astype(F32).reshape(n, 1), invf, sign)


def _rotate_half(x, cos, sin, first_half):
    partner = jnp.where(first_half, pltpu.roll(x, 96, 1), pltpu.roll(x, 32, 1))
    return x * cos + partner * sin


def _qkv_kernel(a_ref, wq_ref, wk_ref, wv_ref, *refs, rope, q_scale, tk):
    if rope:
        cos_ref, sin_ref, q_ref, k_ref, vt_ref = refs
    else:
        q_ref, k_ref, vt_ref = refs
    a = a_ref[...]
    q = jnp.dot(a, wq_ref[...], preferred_element_type=F32)
    k = jnp.dot(a, wk_ref[...], preferred_element_type=F32)
    v = jnp.dot(a, wv_ref[...], preferred_element_type=F32)
    tm, tn = q.shape
    if rope:
        cos = cos_ref[...]
        sin = sin_ref[...]
        lane = lax.broadcasted_iota(jnp.int32, cos.shape, 1)
        first_half = (lane % 64) < 32
        for c in range(tn // V7X_LANES):
            cols = slice(c * V7X_LANES, (c + 1) * V7X_LANES)
            q_ref[:, cols] = (_rotate_half(q[:, cols], cos, sin, first_half) * q_scale).astype(q_ref.dtype)
            k_ref[:, cols] = _rotate_half(k[:, cols], cos, sin, first_half).astype(k_ref.dtype)
    else:
        q_ref[...] = q.astype(q_ref.dtype)
        k_ref[...] = k.astype(k_ref.dtype)
    for hh in range(tn // HEAD_WIDTH):
        for kk in range(tm // tk):
            blk = v[kk * tk:(kk + 1) * tk, hh * HEAD_WIDTH:(hh + 1) * HEAD_WIDTH]
            vt_ref[hh, kk] = blk.T.astype(vt_ref.dtype)


def _qkv(a, w_qkv, *, tk, rope_tables=None, q_scale=1.0):
    n, d = a.shape
    width = w_qkv.shape[1] // 3
    tm = _row_tile(n, 1024)
    tn = 512
    nj = width // tn
    rope = rope_tables is not None
    in_specs = [
        pl.BlockSpec((tm, d), lambda j, i: (i, 0)),
        pl.BlockSpec((d, tn), lambda j, i: (0, j)),
        pl.BlockSpec((d, tn), lambda j, i: (0, j + nj)),
        pl.BlockSpec((d, tn), lambda j, i: (0, j + 2 * nj)),
    ]
    args = [a, w_qkv, w_qkv, w_qkv]
    if rope:
        in_specs += [pl.BlockSpec((tm, V7X_LANES), lambda j, i: (i, 0))] * 2
        args += list(rope_tables)
    heads = width // HEAD_WIDTH
    pipelined = (_nbytes((tm, d), BF16) + 3 * _nbytes((d, tn), BF16) + 3 * _nbytes((tm, tn), BF16)
                 + 3 * _nbytes((tm, tn), F32) + 2 * _nbytes((tm, V7X_LANES), F32))
    return pl.pallas_call(
        functools.partial(_qkv_kernel, rope=rope, q_scale=q_scale, tk=tk),
        grid=(nj, n // tm),
        in_specs=in_specs,
        out_specs=[
            pl.BlockSpec((tm, tn), lambda j, i: (i, j)),
            pl.BlockSpec((tm, tn), lambda j, i: (i, j)),
            pl.BlockSpec((tn // HEAD_WIDTH, tm // tk, HEAD_WIDTH, tk), lambda j, i: (j, i, 0, 0)),
        ],
        out_shape=[
            jax.ShapeDtypeStruct((n, width), BF16),
            jax.ShapeDtypeStruct((n, width), BF16),
            jax.ShapeDtypeStruct((heads, n // tk, HEAD_WIDTH, tk), BF16),
        ],
        compiler_params=_params(("arbitrary", "arbitrary"), pipelined),
        name="qkv_rope" if rope else "qkv",
    )(*args)


def _attn_specs(seq, tq, tk, heads):
    nq = seq // tq
    nk = seq // tk
    w = heads * HEAD_WIDTH
    q_spec = pl.BlockSpec((tq, w), lambda b, h, qi: (b * nq + qi, h))
    k_spec = pl.BlockSpec((seq, w), lambda b, h, qi: (b, h))
    vt_spec = pl.BlockSpec((heads, nk, HEAD_WIDTH, tk), lambda b, h, qi: (h, b, 0, 0))
    o_spec = pl.BlockSpec((tq, w), lambda b, h, qi: (b * nq + qi, h))
    return q_spec, k_spec, vt_spec, o_spec


def _attn_params(seq, tq, tk, heads, score_cols):
    w = heads * HEAD_WIDTH
    pipelined = 2 * _nbytes((tq, w), BF16) + 2 * _nbytes((seq, w), BF16)
    scratch = (_nbytes((2, heads, tk, score_cols), F32) + _nbytes((2, heads, tk, score_cols), BF16)
               + _nbytes((heads, HEAD_WIDTH, score_cols), F32))
    return _params(("arbitrary", "arbitrary", "arbitrary"), pipelined, scratch)


def _head_cols(hh):
    return slice(hh * HEAD_WIDTH, (hh + 1) * HEAD_WIDTH)


def _diff_attn_kernel(q_ref, k_ref, vt_ref, lq1_ref, lk1_ref, lq2_ref, lk2_ref, g_ref, o_ref,
                      qq_ref, s_ref, p_ref, alpha_ref, m_ref, l_ref, acc_ref, *,
                      tq, tk, heads, lambda_init):
    qi = pl.program_id(2)
    feature = lax.broadcasted_iota(jnp.int32, (HEAD_WIDTH, tq), 0)
    for hh in range(heads):
        qt = q_ref[:, _head_cols(hh)].astype(F32).T
        qq_ref[hh, :, :tq] = jnp.where(feature < 64, qt, 0.0).astype(BF16)
        qq_ref[hh, :, tq:] = jnp.where(feature >= 64, qt, 0.0).astype(BF16)
        m_ref[hh] = jnp.full(m_ref.shape[1:], -jnp.inf, F32)
        l_ref[hh] = jnp.zeros(l_ref.shape[1:], F32)
        acc_ref[hh] = jnp.zeros(acc_ref.shape[1:], F32)

    chunks = [(hh, slice(c, c + DIFF_CHUNK)) for hh in range(heads) for c in range(0, 2 * tq, DIFF_CHUNK)]

    def scores_to(slot, kb, hh, cols):
        rows = pl.ds(pl.multiple_of(kb * tk, tk), tk)
        s_ref[slot, hh, cols.start // DIFF_CHUNK] = jnp.dot(
            k_ref[rows, _head_cols(hh)], qq_ref[hh, :, cols], preferred_element_type=F32)

    def accumulate(slot, kb, hh, cols):
        acc_ref[hh, :, cols] = alpha_ref[slot, hh, :, cols] * acc_ref[hh, :, cols] + jnp.dot(
            vt_ref[hh, kb], p_ref[slot, hh, cols.start // DIFF_CHUNK], preferred_element_type=F32)

    def softmax_to(slot, kb, hh, cols, diagonal):
        s = s_ref[slot, hh, cols.start // DIFF_CHUNK]
        if diagonal:
            key = kb * tk + lax.broadcasted_iota(jnp.int32, s.shape, 0)
            qry = qi * tq + cols.start % tq + lax.broadcasted_iota(jnp.int32, s.shape, 1)
            s = jnp.where(key <= qry, s, -jnp.inf)
        m = m_ref[hh, :, cols]
        m_new = jnp.maximum(m, jnp.max(s, axis=0, keepdims=True))
        alpha = jnp.exp2(m - m_new)
        p = jnp.exp2(s - m_new)
        m_ref[hh, :, cols] = m_new
        l_ref[hh, :, cols] = alpha * l_ref[hh, :, cols] + jnp.sum(p, axis=0, keepdims=True)
        alpha_ref[slot, hh, :, cols] = alpha
        p_ref[slot, hh, cols.start // DIFF_CHUNK] = p.astype(BF16)

    def step(slot, kb, prev_kb, diagonal):
        other = 1 - slot
        for hh, cols in chunks:
            if not diagonal:
                accumulate(other, prev_kb, hh, cols)
            scores_to(other, jnp.maximum(kb - 1, 0), hh, cols)
            softmax_to(slot, kb, hh, cols, diagonal)

    nfull = (qi * tq) // tk
    for hh, cols in chunks:
        scores_to(0, nfull, hh, cols)
    step(0, nfull, None, True)

    def body(t, _):
        kb = nfull - 1 - t

        @pl.when(t % 2 == 0)
        def _():
            step(1, kb, kb + 1, False)

        @pl.when(t % 2 == 1)
        def _():
            step(0, kb, kb + 1, False)

        return 0

    lax.fori_loop(0, nfull, body, 0)

    @pl.when(nfull % 2 == 0)
    def _():
        for hh, cols in chunks:
            accumulate(0, 0, hh, cols)

    @pl.when(nfull % 2 == 1)
    def _():
        for hh, cols in chunks:
            accumulate(1, 0, hh, cols)

    lam = (jnp.exp(jnp.sum(lq1_ref[...] * lk1_ref[...], axis=-1, keepdims=True))
           - jnp.exp(jnp.sum(lq2_ref[...] * lk2_ref[...], axis=-1, keepdims=True)) + lambda_init)
    for hh in range(heads):
        o = acc_ref[hh] / l_ref[hh]
        o = o[:, :tq] - lam * o[:, tq:]
        ms = jnp.mean(o * o, axis=0, keepdims=True)
        o = o * lax.rsqrt(ms + EPS) * g_ref[...] * (1.0 - lambda_init)
        o_ref[:, _head_cols(hh)] = o.T.astype(o_ref.dtype)


def _diff_attn(q, k, vt, lq1, lk1, lq2, lk2, subln_g, *, batch, seq, lambda_init):
    n, width = q.shape
    tq, tk, heads = DIFF_TQ, DIFF_TK, DIFF_HEADS_PER_STEP
    q_spec, k_spec, vt_spec, o_spec = _attn_specs(seq, tq, tk, heads)
    dd = lq1.shape[0]
    small = pl.BlockSpec((1, dd), lambda b, h, qi: (0, 0))
    return pl.pallas_call(
        functools.partial(_diff_attn_kernel, tq=tq, tk=tk, heads=heads, lambda_init=lambda_init),
        grid=(batch, width // (heads * HEAD_WIDTH), seq // tq),
        in_specs=[q_spec, k_spec, vt_spec, small, small, small, small,
                  pl.BlockSpec((HEAD_WIDTH, 1), lambda b, h, qi: (0, 0))],
        out_specs=o_spec,
        out_shape=jax.ShapeDtypeStruct((n, width), BF16),
        scratch_shapes=[
            pltpu.VMEM((heads, HEAD_WIDTH, 2 * tq), BF16),
            pltpu.VMEM((2, heads, 2 * tq // DIFF_CHUNK, tk, DIFF_CHUNK), F32),
            pltpu.VMEM((2, heads, 2 * tq // DIFF_CHUNK, tk, DIFF_CHUNK), BF16),
            pltpu.VMEM((2, heads, 1, 2 * tq), F32),
            pltpu.VMEM((heads, 1, 2 * tq), F32),
            pltpu.VMEM((heads, 1, 2 * tq), F32),
            pltpu.VMEM((heads, HEAD_WIDTH, 2 * tq), F32),
        ],
        compiler_params=_attn_params(seq, tq, tk, heads, 2 * tq),
        name="diff_attn",
    )(q, k, vt, lq1.reshape(1, dd), lk1.reshape(1, dd), lq2.reshape(1, dd), lk2.reshape(1, dd),
      subln_g.reshape(HEAD_WIDTH, 1))


def _sb_attn_kernel(q_ref, k_ref, vt_ref, o_ref, qt_ref, later_ref, z_ref, tail_ref, acc_ref, *,
                    tq, tk, heads, scale):
    qi = pl.program_id(2)
    row = lax.broadcasted_iota(jnp.int32, (tk, tk), 0)
    col = lax.broadcasted_iota(jnp.int32, (tk, tk), 1)
    later_ref[...] = jnp.where(col > row, -1.0, 0.0).astype(BF16)
    for hh in range(heads):
        qt_ref[hh] = q_ref[:, _head_cols(hh)].astype(F32).T.astype(BF16)
    tail_ref[...] = jnp.zeros_like(tail_ref)
    acc_ref[...] = jnp.zeros_like(acc_ref)

    def scores_to(slot, kb):
        rows = pl.ds(pl.multiple_of(kb * tk, tk), tk)
        for hh in range(heads):
            z_ref[slot, hh] = jnp.dot(k_ref[rows, _head_cols(hh)], qt_ref[hh],
                                      preferred_element_type=F32) * (scale * LOG2E)

    def visit(blocks):
        neg_later = later_ref[...]
        his, los, colsums = {}, {}, {}
        for slot, _, mask in blocks:
            for hh in range(heads):
                z = z_ref[slot, hh]
                softplus = jnp.maximum(z, 0.0) + jnp.log2(1.0 + jnp.exp2(-jnp.abs(z)))
                z_ref[slot, hh] = z - softplus
                if mask is not None:
                    softplus = jnp.where(mask, softplus, 0.0)
                hi = softplus.astype(BF16)
                his[slot, hh] = hi
                los[slot, hh] = (softplus - hi.astype(F32)).astype(BF16)
                colsums[slot, hh] = jnp.sum(softplus, axis=0, keepdims=True)
        survives = {key: jnp.dot(neg_later, his[key], preferred_element_type=F32)
                    + jnp.dot(neg_later, los[key], preferred_element_type=F32) for key in his}
        weights = {}
        for hh in range(heads):
            tail = tail_ref[hh]
            for slot, _, mask in blocks:
                a = jnp.exp2(z_ref[slot, hh] + survives[slot, hh] + tail)
                if mask is not None:
                    a = jnp.where(mask, a, 0.0)
                weights[slot, hh] = a.astype(BF16)
                tail = tail - colsums[slot, hh]
            tail_ref[hh] = tail
        for slot, kb, _ in blocks:
            for hh in range(heads):
                acc_ref[hh] += jnp.dot(vt_ref[hh, kb], weights[slot, hh], preferred_element_type=F32)

    key = qi * tk + lax.broadcasted_iota(jnp.int32, (tk, tq), 0)
    qry = qi * tq + lax.broadcasted_iota(jnp.int32, (tk, tq), 1)
    before = jnp.maximum(qi - 1, 0)
    scores_to(0, qi)
    scores_to(1, before)
    visit([(0, qi, key < qry), (1, before, jnp.broadcast_to(qi > 0, (tk, tq)))])

    def any_live():
        return jnp.max(tail_ref[...]) > SB_DEAD_LOG2

    def body(c):
        kb = qi - 2 - c[0]
        scores_to(0, kb)
        visit([(0, kb, None)])
        return c[0] + 1, any_live()

    lax.while_loop(lambda c: jnp.logical_and(c[0] < qi - 1, c[1]), body, (jnp.int32(0), any_live()))
    for hh in range(heads):
        o_ref[:, _head_cols(hh)] = acc_ref[hh].T.astype(o_ref.dtype)


def _sb_attn(q, k, vt, *, batch, seq, scale):
    n, width = q.shape
    tq, tk, heads = SB_TQ, SB_TK, SB_HEADS_PER_STEP
    assert tq == tk
    q_spec, k_spec, vt_spec, o_spec = _attn_specs(seq, tq, tk, heads)
    return pl.pallas_call(
        functools.partial(_sb_attn_kernel, tq=tq, tk=tk, heads=heads, scale=scale),
        grid=(batch, width // (heads * HEAD_WIDTH), seq // tq),
        in_specs=[q_spec, k_spec, vt_spec],
        out_specs=o_spec,
        out_shape=jax.ShapeDtypeStruct((n, width), BF16),
        scratch_shapes=[
            pltpu.VMEM((heads, HEAD_WIDTH, tq), BF16),
            pltpu.VMEM((tk, tk), BF16),
            pltpu.VMEM((2, heads, tk, tq), F32),
            pltpu.VMEM((heads, 1, tq), F32),
            pltpu.VMEM((heads, HEAD_WIDTH, tq), F32),
        ],
        compiler_params=_attn_params(seq, tq, tk, heads, tq),
        name="sb_attn",
    )(q, k, vt)


def kernel(x, positions, norm_mix_g, norm_ffn_g, norm_final_g, sc_w_in, sc_conv_w, sc_w_out, sg_w_in, sg_ln_g, sg_ln_b, sg_w_s, sg_b_s, sg_w_out, da_w_qkv, da_lambda_q1, da_lambda_k1, da_lambda_q2, da_lambda_k2, da_subln_g, da_w_out, sb_w_qkv, sb_w_out, ffn_w_gate, ffn_w_up, ffn_conv_w, ffn_conv_b, ffn_w_down):
    batch, seq, d = x.shape
    depth = norm_mix_g.shape[0]
    n = batch * seq
    assert seq % DIFF_TK == 0 and seq % DIFF_TQ == 0 and seq % SB_TQ == 0 and d % HEAD_WIDTH == 0

    def bf(w):
        return w.astype(BF16)

    h = x.reshape(n, d)
    a = _rmsnorm(h, norm_mix_g[0])
    for layer in range(depth):
        mixer, j = layer % 4, layer // 4
        if mixer == 0:
            gb, u = _sc_in(a, bf(sc_w_in[j]))
            y = _sc_conv(u, gb, sc_conv_w[j], seq=seq)
            w_out = sc_w_out[j]
        elif mixer == 1:
            u, v = _sg_in(a, bf(sg_w_in[j]))
            y = _sg_mix(u, v, sg_ln_g[j], sg_ln_b[j], sg_w_s[j], sg_b_s[j])
            w_out = sg_w_out[j]
        elif mixer == 2:
            head_dim = d // DIFF_HEADS // 2
            lambda_init = 0.8 - 0.6 * math.exp(-0.3 * layer)
            tables = _rope_tables(positions, head_dim)
            q, k, vt = _qkv(a, bf(da_w_qkv[j]), tk=DIFF_TK, rope_tables=tables,
                            q_scale=head_dim ** -0.5 * LOG2E)
            y = _diff_attn(q, k, vt, da_lambda_q1[j], da_lambda_k1[j], da_lambda_q2[j], da_lambda_k2[j],
                           da_subln_g[j], batch=batch, seq=seq, lambda_init=lambda_init)
            w_out = da_w_out[j]
        else:
            q, k, vt = _qkv(a, bf(sb_w_qkv[j]), tk=SB_TK)
            y = _sb_attn(q, k, vt, batch=batch, seq=seq, scale=(d // SB_HEADS) ** -0.5)
            w_out = sb_w_out[j]
        h, a = _out_proj(y, bf(w_out), h, norm_ffn_g[layer])
        t = _ffn_up(a, ffn_w_gate, ffn_w_up, layer, ffn_conv_w[layer], ffn_conv_b[layer], seq=seq)
        last = layer == depth - 1
        h, a = _out_proj(t, bf(ffn_w_down[layer]), h, norm_final_g if last else norm_mix_g[layer + 1], final=last)
    return a.reshape(batch, seq, d)
```

```python
import functools
import math

import jax
import jax.numpy as jnp
from jax import lax
from jax.experimental import pallas as pl
from jax.experimental.pallas import tpu as pltpu

F32 = jnp.float32
BF16 = jnp.bfloat16

EPS = 1e-6
ROPE_THETA = 10000.0
CONV_WIDTH = 3
SGU_GROUPS = 16
SGU_CHUNK = 128
DIFF_HEADS = 16
SB_HEADS = 16
HEAD_WIDTH = 128

V7X_LANES = 128
V7X_SUBLANES = 8
V7X_VMEM_BYTES = 64 * 1024 * 1024
INTERNAL_SCRATCH_BYTES = 12 * 1024 * 1024


def _nbytes(shape, dtype):
    return math.prod(shape) * jnp.dtype(dtype).itemsize


def _params(semantics, pipelined_bytes, resident_bytes=0):
    limit = 2 * pipelined_bytes + resident_bytes + INTERNAL_SCRATCH_BYTES
    limit = min(limit, V7X_VMEM_BYTES - 4 * 1024 * 1024)
    return pltpu.CompilerParams(dimension_semantics=semantics, vmem_limit_bytes=int(limit))


def _row_tile(n, want):
    t = min(n, want)
    assert n % t == 0, (n, t)
    return t


def _rms(x, g):
    ms = jnp.mean(x * x, axis=-1, keepdims=True)
    return x * lax.rsqrt(ms + EPS) * g


def _shift_rows(x, prev, k):
    row = lax.broadcasted_iota(jnp.int32, x.shape, 0)
    out = pltpu.roll(x, k, 0)
    for r in range(k):
        out = jnp.where(row == r, prev[V7X_SUBLANES - k + r:V7X_SUBLANES - k + r + 1, :], out)
    return out


def _rmsnorm_kernel(x_ref, g_ref, o_ref):
    o_ref[...] = _rms(x_ref[...], g_ref[...]).astype(o_ref.dtype)


def _rmsnorm(h, g):
    n, d = h.shape
    tm = _row_tile(n, 512)
    return pl.pallas_call(
        _rmsnorm_kernel,
        grid=(n // tm,),
        in_specs=[pl.BlockSpec((tm, d), lambda i: (i, 0)), pl.BlockSpec((1, d), lambda i: (0, 0))],
        out_specs=pl.BlockSpec((tm, d), lambda i: (i, 0)),
        out_shape=jax.ShapeDtypeStruct((n, d), BF16),
        compiler_params=_params(("arbitrary",), _nbytes((tm, d), F32) + _nbytes((tm, d), BF16)),
        name="rmsnorm",
    )(h, g.reshape(1, d))


def _out_proj_kernel(x_ref, w_ref, h_ref, g_ref, *out_refs, emit_h):
    hn = h_ref[...] + jnp.dot(x_ref[...], w_ref[...], preferred_element_type=F32)
    if emit_h:
        out_refs[0][...] = hn
    an_ref = out_refs[-1]
    an_ref[...] = _rms(hn, g_ref[...]).astype(an_ref.dtype)


def _out_proj(x, w, h, g, *, final=False):
    n, k = x.shape
    d = w.shape[1]
    tm = _row_tile(n, 256)
    an_dtype = F32 if final else BF16
    out_shape = [jax.ShapeDtypeStruct((n, d), an_dtype)]
    out_specs = [pl.BlockSpec((tm, d), lambda i: (i, 0))]
    if not final:
        out_shape.insert(0, jax.ShapeDtypeStruct((n, d), F32))
        out_specs.insert(0, pl.BlockSpec((tm, d), lambda i: (i, 0)))
    pipelined = (_nbytes((tm, k), BF16) + 2 * _nbytes((tm, d), F32) + _nbytes((tm, d), an_dtype))
    outs = pl.pallas_call(
        functools.partial(_out_proj_kernel, emit_h=not final),
        grid=(n // tm,),
        in_specs=[
            pl.BlockSpec((tm, k), lambda i: (i, 0)),
            pl.BlockSpec((k, d), lambda i: (0, 0), pipeline_mode=pl.Buffered(1)),
            pl.BlockSpec((tm, d), lambda i: (i, 0)),
            pl.BlockSpec((1, d), lambda i: (0, 0)),
        ],
        out_specs=out_specs,
        out_shape=out_shape,
        compiler_params=_params(("arbitrary",), pipelined, _nbytes((k, d), BF16)),
        name="out_proj",
    )(x, w, h, g.reshape(1, d))
    return (None, outs[0]) if final else (outs[0], outs[1])


def _ffn_up_kernel(a_ref, wg_ref, wu_ref, cw_ref, cb_ref, t_ref, carry_ref, wg_bf_ref, wu_bf_ref, *,
                   tm, seq):
    i = pl.program_id(1)

    @pl.when(i == 0)
    def _():
        wg_bf_ref[...] = wg_ref[...].astype(BF16)
        wu_bf_ref[...] = wu_ref[...].astype(BF16)

    @pl.when((i * tm) % seq == 0)
    def _():
        carry_ref[...] = jnp.zeros_like(carry_ref)

    a = a_ref[...]
    g = jnp.dot(a, wg_bf_ref[...], preferred_element_type=F32)
    up = jnp.dot(a, wu_bf_ref[...], preferred_element_type=F32)
    prev = carry_ref[...]
    cw = cw_ref[...]
    conv = (cw[0:1, :] * _shift_rows(g, prev, 2) + cw[1:2, :] * _shift_rows(g, prev, 1)
            + cw[2:3, :] * g + cb_ref[...])
    t_ref[...] = (conv * jax.nn.sigmoid(conv) * up).astype(t_ref.dtype)
    carry_ref[...] = g[tm - V7X_SUBLANES:, :]


def _ffn_up(a, w_gate, w_up, layer, conv_w, conv_b, *, seq):
    n, d = a.shape
    f = w_gate.shape[2]
    tm = _row_tile(seq, 1024)
    tn = 512
    assert f % tn == 0
    pipelined = (_nbytes((tm, d), BF16) + 2 * _nbytes((d, tn), F32) + _nbytes((tm, tn), BF16)
                 + 2 * _nbytes((tm, tn), F32))
    w_spec = pl.BlockSpec((None, d, tn), lambda j, i: (layer, 0, j))
    return pl.pallas_call(
        functools.partial(_ffn_up_kernel, tm=tm, seq=seq),
        grid=(f // tn, n // tm),
        in_specs=[
            pl.BlockSpec((tm, d), lambda j, i: (i, 0)),
            w_spec,
            w_spec,
            pl.BlockSpec((CONV_WIDTH, tn), lambda j, i: (0, j)),
            pl.BlockSpec((1, tn), lambda j, i: (0, j)),
        ],
        out_specs=pl.BlockSpec((tm, tn), lambda j, i: (i, j)),
        out_shape=jax.ShapeDtypeStruct((n, f), BF16),
        scratch_shapes=[pltpu.VMEM((V7X_SUBLANES, tn), F32), pltpu.VMEM((d, tn), BF16),
                        pltpu.VMEM((d, tn), BF16)],
        compiler_params=_params(("arbitrary", "arbitrary"), pipelined, 2 * _nbytes((d, tn), BF16)),
        name="ffn_up",
    )(a, w_gate, w_up, conv_w, conv_b.reshape(1, f))


def _sc_in_kernel(a_ref, wb_ref, wc_ref, wx_ref, gb_ref, u_ref):
    a = a_ref[...]
    gb_ref[...] = jnp.dot(a, wb_ref[...], preferred_element_type=F32)
    gc = jnp.dot(a, wc_ref[...], preferred_element_type=F32)
    xi = jnp.dot(a, wx_ref[...], preferred_element_type=F32)
    u_ref[...] = gc * xi


def _sc_in(a, w_in):
    n, d = a.shape
    tm = _row_tile(n, 1024)
    tn = 512
    nj = d // tn
    pipelined = _nbytes((tm, d), BF16) + 3 * _nbytes((d, tn), BF16) + 3 * _nbytes((tm, tn), F32)
    return pl.pallas_call(
        _sc_in_kernel,
        grid=(nj, n // tm),
        in_specs=[
            pl.BlockSpec((tm, d), lambda j, i: (i, 0)),
            pl.BlockSpec((d, tn), lambda j, i: (0, j)),
            pl.BlockSpec((d, tn), lambda j, i: (0, j + nj)),
            pl.BlockSpec((d, tn), lambda j, i: (0, j + 2 * nj)),
        ],
        out_specs=[pl.BlockSpec((tm, tn), lambda j, i: (i, j))] * 2,
        out_shape=[jax.ShapeDtypeStruct((n, d), F32)] * 2,
        compiler_params=_params(("arbitrary", "arbitrary"), pipelined),
        name="sc_in",
    )(a, w_in, w_in, w_in)


def _sc_conv_kernel(u_ref, halo_ref, gb_ref, cw_ref, y_ref, *, tm, seq):
    i = pl.program_id(0)
    u = u_ref[...]
    prev = jnp.where((i * tm) % seq == 0, 0.0, halo_ref[...])
    cw = cw_ref[...]
    conv = cw[0:1, :] * _shift_rows(u, prev, 2) + cw[1:2, :] * _shift_rows(u, prev, 1) + cw[2:3, :] * u
    y_ref[...] = (gb_ref[...] * conv).astype(y_ref.dtype)


def _sc_conv(u, gb, conv_w, *, seq):
    n, d = u.shape
    tm = _row_tile(seq, 512)
    hb = tm // V7X_SUBLANES
    pipelined = 2 * _nbytes((tm, d), F32) + _nbytes((tm, d), BF16) + _nbytes((tm, d), F32)
    return pl.pallas_call(
        functools.partial(_sc_conv_kernel, tm=tm, seq=seq),
        grid=(n // tm,),
        in_specs=[
            pl.BlockSpec((tm, d), lambda i: (i, 0)),
            pl.BlockSpec((V7X_SUBLANES, d), lambda i: (jnp.maximum(i * hb - 1, 0), 0)),
            pl.BlockSpec((tm, d), lambda i: (i, 0)),
            pl.BlockSpec((CONV_WIDTH, d), lambda i: (0, 0)),
        ],
        out_specs=pl.BlockSpec((tm, d), lambda i: (i, 0)),
        out_shape=jax.ShapeDtypeStruct((n, d), BF16),
        compiler_params=_params(("arbitrary",), pipelined),
        name="sc_conv",
    )(u, u, gb, conv_w)


def _sg_in_kernel(a_ref, wu_ref, wv_ref, u_ref, v_ref):
    a = a_ref[...]
    u_ref[...] = jax.nn.gelu(jnp.dot(a, wu_ref[...], preferred_element_type=F32))
    v_ref[...] = jax.nn.gelu(jnp.dot(a, wv_ref[...], preferred_element_type=F32))


def _sg_in(a, w_in):
    n, d = a.shape
    width = w_in.shape[1] // 2
    tm = _row_tile(n, 1024)
    tn = 512
    nj = width // tn
    pipelined = _nbytes((tm, d), BF16) + 2 * _nbytes((d, tn), BF16) + 2 * _nbytes((tm, tn), F32)
    return pl.pallas_call(
        _sg_in_kernel,
        grid=(nj, n // tm),
        in_specs=[
            pl.BlockSpec((tm, d), lambda j, i: (i, 0)),
            pl.BlockSpec((d, tn), lambda j, i: (0, j)),
            pl.BlockSpec((d, tn), lambda j, i: (0, j + nj)),
        ],
        out_specs=[pl.BlockSpec((tm, tn), lambda j, i: (i, j))] * 2,
        out_shape=[jax.ShapeDtypeStruct((n, width), F32)] * 2,
        compiler_params=_params(("arbitrary", "arbitrary"), pipelined),
        name="sg_in",
    )(a, w_in, w_in)


def _sg_mix_kernel(u_ref, v_ref, lng_ref, lnb_ref, ws_ref, bs_ref, y_ref, vn_ref, *, tm):
    v = v_ref[...]
    mu = jnp.mean(v, axis=-1, keepdims=True)
    vc = v - mu
    var = jnp.mean(vc * vc, axis=-1, keepdims=True)
    vn_ref[...] = (vc * lax.rsqrt(var + EPS) * lng_ref[...] + lnb_ref[...]).astype(vn_ref.dtype)
    t = SGU_CHUNK
    nchunk = tm // t
    row = lax.broadcasted_iota(jnp.int32, (t, t), 0)
    col = lax.broadcasted_iota(jnp.int32, (t, t), 1)
    bs = bs_ref[...]
    for g in range(SGU_GROUPS):
        cols = slice(g * t, (g + 1) * t)
        w = jnp.where(row >= col, ws_ref[g], 0.0).astype(BF16)
        rhs = jnp.concatenate([vn_ref[c * t:(c + 1) * t, cols] for c in range(nchunk)], axis=1)
        mixed = jnp.dot(w, rhs, preferred_element_type=F32) + bs[:, g:g + 1]
        for c in range(nchunk):
            rows = slice(c * t, (c + 1) * t)
            y_ref[rows, cols] = (u_ref[rows, cols] * mixed[:, c * t:(c + 1) * t]).astype(y_ref.dtype)


def _sg_mix(u, v, ln_g, ln_b, w_s, b_s):
    n, width = u.shape
    tm = _row_tile(n, 512)
    assert tm % SGU_CHUNK == 0
    pipelined = 2 * _nbytes((tm, width), F32) + _nbytes((tm, width), BF16)
    resident = 2 * _nbytes(w_s.shape, F32) + _nbytes((tm, width), BF16)
    return pl.pallas_call(
        functools.partial(_sg_mix_kernel, tm=tm),
        grid=(n // tm,),
        in_specs=[
            pl.BlockSpec((tm, width), lambda i: (i, 0)),
            pl.BlockSpec((tm, width), lambda i: (i, 0)),
            pl.BlockSpec((1, width), lambda i: (0, 0)),
            pl.BlockSpec((1, width), lambda i: (0, 0)),
            pl.BlockSpec(w_s.shape, lambda i: (0, 0, 0)),
            pl.BlockSpec((SGU_CHUNK, SGU_GROUPS), lambda i: (0, 0)),
        ],
        out_specs=pl.BlockSpec((tm, width), lambda i: (i, 0)),
        out_shape=jax.ShapeDtypeStruct((n, width), BF16),
        scratch_shapes=[pltpu.VMEM((tm, width), BF16)],
        compiler_params=_params(("arbitrary",), pipelined, resident),
        name="sg_mix",
    )(u, v, ln_g.reshape(1, width), ln_b.reshape(1, width), w_s, b_s.T)


DIFF_TQ = 512
DIFF_TK = 512
DIFF_HEADS_PER_STEP = 2
DIFF_CHUNK = 256
SB_TQ = 256
SB_TK = 256
SB_HEADS_PER_STEP = 4
SB_DEAD_LOG2 = -160.0
LOG2E = 1.4426950408889634


def _rope_table_kernel(pos_ref, invf_ref, sign_ref, cos_ref, sin_ref):
    ang = pos_ref[...] * invf_ref[...]
    cos_ref[...] = jnp.cos(ang)
    sin_ref[...] = jnp.sin(ang) * sign_ref[...]


def _rope_tables(positions, head_dim):
    n = positions.size
    half = head_dim // 2
    inv_freq = 1.0 / (ROPE_THETA ** (jnp.arange(0, head_dim, 2, dtype=F32) / head_dim))
    lane = jnp.arange(V7X_LANES)
    invf = inv_freq[lane % half].reshape(1, V7X_LANES)
    sign = jnp.where(lane % head_dim < half, -1.0, 1.0).astype(F32).reshape(1, V7X_LANES)
    tm = _row_tile(n, 1024)
    return pl.pallas_call(
        _rope_table_kernel,
        grid=(n // tm,),
        in_specs=[
            pl.BlockSpec((tm, 1), lambda i: (i, 0)),
            pl.BlockSpec((1, V7X_LANES), lambda i: (0, 0)),
            pl.BlockSpec((1, V7X_LANES), lambda i: (0, 0)),
        ],
        out_specs=[pl.BlockSpec((tm, V7X_LANES), lambda i: (i, 0))] * 2,
        out_shape=[jax.ShapeDtypeStruct((n, V7X_LANES), F32)] * 2,
        compiler_params=_params(("arbitrary",), 3 * _nbytes((tm, V7X_LANES), F32)),
        name="rope_tables",
    )(positions.astype(F32).reshape(n, 1), invf, sign)


def _rotate_half(x, cos, sin, first_half):
    partner = jnp.where(first_half, pltpu.roll(x, 96, 1), pltpu.roll(x, 32, 1))
    return x * cos + partner * sin


def _qkv_kernel(a_ref, wq_ref, wk_ref, wv_ref, *refs, rope, q_scale, tk):
    if rope:
        cos_ref, sin_ref, q_ref, k_ref, vt_ref = refs
    else:
        q_ref, k_ref, vt_ref = refs
    a = a_ref[...]
    q = jnp.dot(a, wq_ref[...], preferred_element_type=F32)
    k = jnp.dot(a, wk_ref[...], preferred_element_type=F32)
    v = jnp.dot(a, wv_ref[...], preferred_element_type=F32)
    tm, tn = q.shape
    if rope:
        cos = cos_ref[...]
        sin = sin_ref[...]
        lane = lax.broadcasted_iota(jnp.int32, cos.shape, 1)
        first_half = (lane % 64) < 32
        for c in range(tn // V7X_LANES):
            cols = slice(c * V7X_LANES, (c + 1) * V7X_LANES)
            q_ref[:, cols] = (_rotate_half(q[:, cols], cos, sin, first_half) * q_scale).astype(q_ref.dtype)
            k_ref[:, cols] = _rotate_half(k[:, cols], cos, sin, first_half).astype(k_ref.dtype)
    else:
        q_ref[...] = q.astype(q_ref.dtype)
        k_ref[...] = k.astype(k_ref.dtype)
    for hh in range(tn // HEAD_WIDTH):
        for kk in range(tm // tk):
            blk = v[kk * tk:(kk + 1) * tk, hh * HEAD_WIDTH:(hh + 1) * HEAD_WIDTH]
            vt_ref[hh, kk] = blk.T.astype(vt_ref.dtype)


def _qkv(a, w_qkv, *, tk, rope_tables=None, q_scale=1.0):
    n, d = a.shape
    width = w_qkv.shape[1] // 3
    tm = _row_tile(n, 1024)
    tn = 512
    nj = width // tn
    rope = rope_tables is not None
    in_specs = [
        pl.BlockSpec((tm, d), lambda j, i: (i, 0)),
        pl.BlockSpec((d, tn), lambda j, i: (0, j)),
        pl.BlockSpec((d, tn), lambda j, i: (0, j + nj)),
        pl.BlockSpec((d, tn), lambda j, i: (0, j + 2 * nj)),
    ]
    args = [a, w_qkv, w_qkv, w_qkv]
    if rope:
        in_specs += [pl.BlockSpec((tm, V7X_LANES), lambda j, i: (i, 0))] * 2
        args += list(rope_tables)
    heads = width // HEAD_WIDTH
    pipelined = (_nbytes((tm, d), BF16) + 3 * _nbytes((d, tn), BF16) + 3 * _nbytes((tm, tn), BF16)
                 + 3 * _nbytes((tm, tn), F32) + 2 * _nbytes((tm, V7X_LANES), F32))
    return pl.pallas_call(
        functools.partial(_qkv_kernel, rope=rope, q_scale=q_scale, tk=tk),
        grid=(nj, n // tm),
        in_specs=in_specs,
        out_specs=[
            pl.BlockSpec((tm, tn), lambda j, i: (i, j)),
            pl.BlockSpec((tm, tn), lambda j, i: (i, j)),
            pl.BlockSpec((tn // HEAD_WIDTH, tm // tk, HEAD_WIDTH, tk), lambda j, i: (j, i, 0, 0)),
        ],
        out_shape=[
            jax.ShapeDtypeStruct((n, width), BF16),
            jax.ShapeDtypeStruct((n, width), BF16),
            jax.ShapeDtypeStruct((heads, n // tk, HEAD_WIDTH, tk), BF16),
        ],
        compiler_params=_params(("arbitrary", "arbitrary"), pipelined),
        name="qkv_rope" if rope else "qkv",
    )(*args)


def _attn_specs(seq, tq, tk, heads):
    nq = seq // tq
    nk = seq // tk
    w = heads * HEAD_WIDTH
    q_spec = pl.BlockSpec((tq, w), lambda b, h, qi: (b * nq + qi, h))
    k_specs = [pl.BlockSpec((seq, HEAD_WIDTH), functools.partial(lambda b, h, qi, hh: (b, h * heads + hh), hh=hh))
               for hh in range(heads)]
    vt_spec = pl.BlockSpec((heads, nk, HEAD_WIDTH, tk), lambda b, h, qi: (h, b, 0, 0))
    o_spec = pl.BlockSpec((tq, w), lambda b, h, qi: (b * nq + qi, h))
    return q_spec, k_specs, vt_spec, o_spec


def _attn_params(seq, tq, tk, heads, score_cols):
    w = heads * HEAD_WIDTH
    pipelined = 2 * _nbytes((tq, w), BF16) + 2 * _nbytes((seq, w), BF16)
    scratch = (_nbytes((2, heads, tk, score_cols), F32) + _nbytes((2, heads, tk, score_cols), BF16)
               + _nbytes((heads, HEAD_WIDTH, score_cols), F32))
    return _params(("arbitrary", "arbitrary", "arbitrary"), pipelined, scratch)


def _head_cols(hh):
    return slice(hh * HEAD_WIDTH, (hh + 1) * HEAD_WIDTH)


def _diff_attn_kernel(q_ref, *refs, tq, tk, heads, lambda_init):
    k_refs = refs[:heads]
    (vt_ref, lq1_ref, lk1_ref, lq2_ref, lk2_ref, g_ref, o_ref,
     qq_ref, s_ref, p_ref, alpha_ref, m_ref, l_ref, acc_ref) = refs[heads:]
    qi = pl.program_id(2)
    feature = lax.broadcasted_iota(jnp.int32, (HEAD_WIDTH, tq), 0)
    for hh in range(heads):
        qt = q_ref[:, _head_cols(hh)].astype(F32).T
        qq_ref[hh, :, :tq] = jnp.where(feature < 64, qt, 0.0).astype(BF16)
        qq_ref[hh, :, tq:] = jnp.where(feature >= 64, qt, 0.0).astype(BF16)
        m_ref[hh] = jnp.full(m_ref.shape[1:], -jnp.inf, F32)
        l_ref[hh] = jnp.zeros(l_ref.shape[1:], F32)
        acc_ref[hh] = jnp.zeros(acc_ref.shape[1:], F32)

    chunks = [(hh, slice(c, c + DIFF_CHUNK)) for hh in range(heads) for c in range(0, 2 * tq, DIFF_CHUNK)]

    def scores_to(slot, kb, hh, cols):
        rows = pl.ds(pl.multiple_of(kb * tk, tk), tk)
        s_ref[slot, hh, cols.start // DIFF_CHUNK] = jnp.dot(
            k_refs[hh][rows, :], qq_ref[hh, :, cols], preferred_element_type=F32)

    def accumulate(slot, kb, hh, cols):
        ci = cols.start // DIFF_CHUNK
        acc_ref[hh, ci] = alpha_ref[slot, hh, :, cols] * acc_ref[hh, ci] + jnp.dot(
            vt_ref[hh, kb], p_ref[slot, hh, cols.start // DIFF_CHUNK], preferred_element_type=F32)

    def softmax_to(slot, kb, hh, cols, diagonal):
        s = s_ref[slot, hh, cols.start // DIFF_CHUNK]
        if diagonal:
            key = kb * tk + lax.broadcasted_iota(jnp.int32, s.shape, 0)
            qry = qi * tq + cols.start % tq + lax.broadcasted_iota(jnp.int32, s.shape, 1)
            s = jnp.where(key <= qry, s, -jnp.inf)
        m = m_ref[hh, :, cols]
        m_new = jnp.maximum(m, jnp.max(s, axis=0, keepdims=True))
        alpha = jnp.exp2(m - m_new)
        p = jnp.exp2(s - m_new)
        m_ref[hh, :, cols] = m_new
        l_ref[hh, :, cols] = alpha * l_ref[hh, :, cols] + jnp.sum(p, axis=0, keepdims=True)
        alpha_ref[slot, hh, :, cols] = alpha
        p_ref[slot, hh, cols.start // DIFF_CHUNK] = p.astype(BF16)

    def step(slot, kb, prev_kb, diagonal):
        other = 1 - slot
        for hh, cols in chunks:
            if not diagonal:
                accumulate(other, prev_kb, hh, cols)
            scores_to(other, jnp.maximum(kb - 1, 0), hh, cols)
            softmax_to(slot, kb, hh, cols, diagonal)

    nfull = (qi * tq) // tk
    for hh, cols in chunks:
        scores_to(0, nfull, hh, cols)
    step(0, nfull, None, True)

    def body(t, _):
        kb = nfull - 1 - t

        @pl.when(t % 2 == 0)
        def _():
            step(1, kb, kb + 1, False)

        @pl.when(t % 2 == 1)
        def _():
            step(0, kb, kb + 1, False)

        return 0

    lax.fori_loop(0, nfull, body, 0)

    @pl.when(nfull % 2 == 0)
    def _():
        for hh, cols in chunks:
            accumulate(0, 0, hh, cols)

    @pl.when(nfull % 2 == 1)
    def _():
        for hh, cols in chunks:
            accumulate(1, 0, hh, cols)

    lam = (jnp.exp(jnp.sum(lq1_ref[...] * lk1_ref[...], axis=-1, keepdims=True))
           - jnp.exp(jnp.sum(lq2_ref[...] * lk2_ref[...], axis=-1, keepdims=True)) + lambda_init)
    for hh in range(heads):
        for c in range(0, tq, DIFF_CHUNK):
            first, second = slice(c, c + DIFF_CHUNK), slice(tq + c, tq + c + DIFF_CHUNK)
            o = (acc_ref[hh, first.start // DIFF_CHUNK] / l_ref[hh, :, first]
                 - lam * (acc_ref[hh, second.start // DIFF_CHUNK] / l_ref[hh, :, second]))
            ms = jnp.mean(o * o, axis=0, keepdims=True)
            o = o * lax.rsqrt(ms + EPS) * g_ref[...] * (1.0 - lambda_init)
            o_ref[first, _head_cols(hh)] = o.T.astype(o_ref.dtype)


def _diff_attn(q, k, vt, lq1, lk1, lq2, lk2, subln_g, *, batch, seq, lambda_init):
    n, width = q.shape
    tq, tk, heads = DIFF_TQ, DIFF_TK, DIFF_HEADS_PER_STEP
    q_spec, k_specs, vt_spec, o_spec = _attn_specs(seq, tq, tk, heads)
    dd = lq1.shape[0]
    small = pl.BlockSpec((1, dd), lambda b, h, qi: (0, 0))
    return pl.pallas_call(
        functools.partial(_diff_attn_kernel, tq=tq, tk=tk, heads=heads, lambda_init=lambda_init),
        grid=(batch, width // (heads * HEAD_WIDTH), seq // tq),
        in_specs=[q_spec, *k_specs, vt_spec, small, small, small, small,
                  pl.BlockSpec((HEAD_WIDTH, 1), lambda b, h, qi: (0, 0))],
        out_specs=o_spec,
        out_shape=jax.ShapeDtypeStruct((n, width), BF16),
        scratch_shapes=[
            pltpu.VMEM((heads, HEAD_WIDTH, 2 * tq), BF16),
            pltpu.VMEM((2, heads, 2 * tq // DIFF_CHUNK, tk, DIFF_CHUNK), F32),
            pltpu.VMEM((2, heads, 2 * tq // DIFF_CHUNK, tk, DIFF_CHUNK), BF16),
            pltpu.VMEM((2, heads, 1, 2 * tq), F32),
            pltpu.VMEM((heads, 1, 2 * tq), F32),
            pltpu.VMEM((heads, 1, 2 * tq), F32),
            pltpu.VMEM((heads, 2 * tq // DIFF_CHUNK, HEAD_WIDTH, DIFF_CHUNK), F32),
        ],
        compiler_params=_attn_params(seq, tq, tk, heads, 2 * tq),
        name="diff_attn",
    )(q, *[k] * heads, vt, lq1.reshape(1, dd), lk1.reshape(1, dd), lq2.reshape(1, dd), lk2.reshape(1, dd),
      subln_g.reshape(HEAD_WIDTH, 1))


def _sb_attn_kernel(q_ref, *refs, tq, tk, heads, scale):
    k_refs = refs[:heads]
    vt_ref, o_ref, qt_ref, later_ref, z_ref, tail_ref, acc_ref = refs[heads:]
    qi = pl.program_id(2)
    row = lax.broadcasted_iota(jnp.int32, (tk, tk), 0)
    col = lax.broadcasted_iota(jnp.int32, (tk, tk), 1)
    later_ref[...] = jnp.where(col > row, -1.0, 0.0).astype(BF16)
    for hh in range(heads):
        qt_ref[hh] = q_ref[:, _head_cols(hh)].astype(F32).T.astype(BF16)
    tail_ref[...] = jnp.zeros_like(tail_ref)
    acc_ref[...] = jnp.zeros_like(acc_ref)

    def scores_to(slot, kb):
        rows = pl.ds(pl.multiple_of(kb * tk, tk), tk)
        for hh in range(heads):
            z_ref[slot, hh] = jnp.dot(k_refs[hh][rows, :], qt_ref[hh],
                                      preferred_element_type=F32) * (scale * LOG2E)

    def visit(blocks):
        neg_later = later_ref[...]
        his, los, colsums = {}, {}, {}
        for slot, _, mask in blocks:
            for hh in range(heads):
                z = z_ref[slot, hh]
                softplus = jnp.maximum(z, 0.0) + jnp.log2(1.0 + jnp.exp2(-jnp.abs(z)))
                z_ref[slot, hh] = z - softplus
                if mask is not None:
                    softplus = jnp.where(mask, softplus, 0.0)
                hi = softplus.astype(BF16)
                his[slot, hh] = hi
                los[slot, hh] = (softplus - hi.astype(F32)).astype(BF16)
                colsums[slot, hh] = jnp.sum(softplus, axis=0, keepdims=True)
        survives = {key: jnp.dot(neg_later, his[key], preferred_element_type=F32)
                    + jnp.dot(neg_later, los[key], preferred_element_type=F32) for key in his}
        weights = {}
        for hh in range(heads):
            tail = tail_ref[hh]
            for slot, _, mask in blocks:
                a = jnp.exp2(z_ref[slot, hh] + survives[slot, hh] + tail)
                if mask is not None:
                    a = jnp.where(mask, a, 0.0)
                weights[slot, hh] = a.astype(BF16)
                tail = tail - colsums[slot, hh]
            tail_ref[hh] = tail
        for slot, kb, _ in blocks:
            for hh in range(heads):
                acc_ref[hh] += jnp.dot(vt_ref[hh, kb], weights[slot, hh], preferred_element_type=F32)

    key = qi * tk + lax.broadcasted_iota(jnp.int32, (tk, tq), 0)
    qry = qi * tq + lax.broadcasted_iota(jnp.int32, (tk, tq), 1)
    before = jnp.maximum(qi - 1, 0)
    scores_to(0, qi)
    scores_to(1, before)
    visit([(0, qi, key < qry), (1, before, jnp.broadcast_to(qi > 0, (tk, tq)))])

    def any_live():
        return jnp.max(tail_ref[...]) > SB_DEAD_LOG2

    def body(c):
        kb = qi - 2 - c[0]
        scores_to(0, kb)
        visit([(0, kb, None)])
        return c[0] + 1, any_live()

    lax.while_loop(lambda c: jnp.logical_and(c[0] < qi - 1, c[1]), body, (jnp.int32(0), any_live()))
    for hh in range(heads):
        o_ref[:, _head_cols(hh)] = acc_ref[hh].T.astype(o_ref.dtype)


def _sb_attn(q, k, vt, *, batch, seq, scale):
    n, width = q.shape
    tq, tk, heads = SB_TQ, SB_TK, SB_HEADS_PER_STEP
    assert tq == tk
    q_spec, k_specs, vt_spec, o_spec = _attn_specs(seq, tq, tk, heads)
    return pl.pallas_call(
        functools.partial(_sb_attn_kernel, tq=tq, tk=tk, heads=heads, scale=scale),
        grid=(batch, width // (heads * HEAD_WIDTH), seq // tq),
        in_specs=[q_spec, *k_specs, vt_spec],
        out_specs=o_spec,
        out_shape=jax.ShapeDtypeStruct((n, width), BF16),
        scratch_shapes=[
            pltpu.VMEM((heads, HEAD_WIDTH, tq), BF16),
            pltpu.VMEM((tk, tk), BF16),
            pltpu.VMEM((2, heads, tk, tq), F32),
            pltpu.VMEM((heads, 1, tq), F32),
            pltpu.VMEM((heads, HEAD_WIDTH, tq), F32),
        ],
        compiler_params=_attn_params(seq, tq, tk, heads, tq),
        name="sb_attn",
    )(q, *[k] * heads, vt)


def kernel(x, positions, norm_mix_g, norm_ffn_g, norm_final_g, sc_w_in, sc_conv_w, sc_w_out, sg_w_in, sg_ln_g, sg_ln_b, sg_w_s, sg_b_s, sg_w_out, da_w_qkv, da_lambda_q1, da_lambda_k1, da_lambda_q2, da_lambda_k2, da_subln_g, da_w_out, sb_w_qkv, sb_w_out, ffn_w_gate, ffn_w_up, ffn_conv_w, ffn_conv_b, ffn_w_down):
    batch, seq, d = x.shape
    depth = norm_mix_g.shape[0]
    n = batch * seq
    assert seq % DIFF_TK == 0 and seq % DIFF_TQ == 0 and seq % SB_TQ == 0 and d % HEAD_WIDTH == 0

    def bf(w):
        return w.astype(BF16)

    h = x.reshape(n, d)
    a = _rmsnorm(h, norm_mix_g[0])
    for layer in range(depth):
        mixer, j = layer % 4, layer // 4
        if mixer == 0:
            gb, u = _sc_in(a, bf(sc_w_in[j]))
            y = _sc_conv(u, gb, sc_conv_w[j], seq=seq)
            w_out = sc_w_out[j]
        elif mixer == 1:
            u, v = _sg_in(a, bf(sg_w_in[j]))
            y = _sg_mix(u, v, sg_ln_g[j], sg_ln_b[j], sg_w_s[j], sg_b_s[j])
            w_out = sg_w_out[j]
        elif mixer == 2:
            head_dim = d // DIFF_HEADS // 2
            lambda_init = 0.8 - 0.6 * math.exp(-0.3 * layer)
            tables = _rope_tables(positions, head_dim)
            q, k, vt = _qkv(a, bf(da_w_qkv[j]), tk=DIFF_TK, rope_tables=tables,
                            q_scale=head_dim ** -0.5 * LOG2E)
            y = _diff_attn(q, k, vt, da_lambda_q1[j], da_lambda_k1[j], da_lambda_q2[j], da_lambda_k2[j],
                           da_subln_g[j], batch=batch, seq=seq, lambda_init=lambda_init)
            w_out = da_w_out[j]
        else:
            q, k, vt = _qkv(a, bf(sb_w_qkv[j]), tk=SB_TK)
            y = _sb_attn(q, k, vt, batch=batch, seq=seq, scale=(d // SB_HEADS) ** -0.5)
            w_out = sb_w_out[j]
        h, a = _out_proj(y, bf(w_out), h, norm_ffn_g[layer])
        t = _ffn_up(a, ffn_w_gate, ffn_w_up, layer, ffn_conv_w[layer], ffn_conv_b[layer], seq=seq)
        last = layer == depth - 1
        h, a = _out_proj(t, bf(ffn_w_down[layer]), h, norm_final_g if last else norm_mix_g[layer + 1], final=last)
    return a.reshape(batch, seq, d)
```

```python
import functools
import math
from typing import Callable, NamedTuple, Sequence

import jax
import jax.numpy as jnp
from jax import lax
from jax.experimental import pallas as pl
from jax.experimental.pallas import tpu as pltpu

F32 = jnp.float32
BF16 = jnp.bfloat16

EPS = 1e-6
ROPE_THETA = 10000.0
CONV_WIDTH = 3
SGU_GROUPS = 16
SGU_CHUNK = 128
DIFF_HEADS = 16
SB_HEADS = 16
HEAD_WIDTH = 128

V7X_LANES = 128
V7X_SUBLANES = 8
V7X_VMEM_BYTES = 64 * 1024 * 1024
INTERNAL_SCRATCH_BYTES = 12 * 1024 * 1024


def _nbytes(shape, dtype):
    return math.prod(shape) * jnp.dtype(dtype).itemsize


def _params(semantics, pipelined_bytes, resident_bytes=0):
    limit = 2 * pipelined_bytes + resident_bytes + INTERNAL_SCRATCH_BYTES
    limit = min(limit, V7X_VMEM_BYTES - 4 * 1024 * 1024)
    return pltpu.CompilerParams(dimension_semantics=semantics, vmem_limit_bytes=int(limit))


def _row_tile(n, want):
    t = min(n, want)
    assert n % t == 0, (n, t)
    return t


def _rms(x, g):
    ms = jnp.mean(x * x, axis=-1, keepdims=True)
    return x * lax.rsqrt(ms + EPS) * g


def _shift_rows(x, prev, k):
    row = lax.broadcasted_iota(jnp.int32, x.shape, 0)
    out = pltpu.roll(x, k, 0)
    for r in range(k):
        out = jnp.where(row == r, prev[V7X_SUBLANES - k + r:V7X_SUBLANES - k + r + 1, :], out)
    return out


def _rmsnorm_kernel(x_ref, g_ref, o_ref):
    o_ref[...] = _rms(x_ref[...], g_ref[...]).astype(o_ref.dtype)


def _rmsnorm(h, g):
    n, d = h.shape
    tm = _row_tile(n, 512)
    return pl.pallas_call(
        _rmsnorm_kernel,
        grid=(n // tm,),
        in_specs=[pl.BlockSpec((tm, d), lambda i: (i, 0)), pl.BlockSpec((1, d), lambda i: (0, 0))],
        out_specs=pl.BlockSpec((tm, d), lambda i: (i, 0)),
        out_shape=jax.ShapeDtypeStruct((n, d), BF16),
        compiler_params=_params(("arbitrary",), _nbytes((tm, d), F32) + _nbytes((tm, d), BF16)),
        name="rmsnorm",
    )(h, g.reshape(1, d))


class _Source(NamedTuple):
    make_x: Callable
    inputs: Sequence
    specs: Callable
    row_bytes: int
    scratch: Callable
    resident_bytes: Callable
    name: str


def _plain_source(x):
    k = x.shape[1]
    return _Source(lambda tm: lambda x_ref: x_ref[...], [x],
                   lambda tm: [pl.BlockSpec((tm, k), lambda i: (i, 0))],
                   k * 2, lambda tm: [], lambda tm: 0, "out_proj")


def _out_proj_kernel(*refs, n_src, n_scratch, make_x, emit_h):
    src = refs[:n_src]
    w_ref, h_ref, g_ref = refs[n_src:n_src + 3]
    out_refs = refs[n_src + 3:len(refs) - n_scratch]
    scratch = refs[len(refs) - n_scratch:]
    hn = h_ref[...] + jnp.dot(make_x(*src, *scratch), w_ref[...], preferred_element_type=F32)
    if emit_h:
        out_refs[0][...] = hn
    an_ref = out_refs[-1]
    an_ref[...] = _rms(hn, g_ref[...]).astype(an_ref.dtype)


def _out_proj(source, w, h, g, *, final=False):
    if not isinstance(source, _Source):
        source = _plain_source(source)
    n, d = h.shape
    k = w.shape[0]
    an_dtype = F32 if final else BF16

    def pipelined_bytes(tm):
        return tm * source.row_bytes + 2 * _nbytes((tm, d), F32) + _nbytes((tm, d), an_dtype)

    def resident_bytes(tm):
        return _nbytes((k, d), BF16) + source.resident_bytes(tm)

    budget = V7X_VMEM_BYTES - 4 * 1024 * 1024 - INTERNAL_SCRATCH_BYTES
    tm = next(t for t in (512, 256, 128)
              if n % t == 0 and 2 * pipelined_bytes(t) + resident_bytes(t) <= budget)
    out_shape = [jax.ShapeDtypeStruct((n, d), an_dtype)]
    out_specs = [pl.BlockSpec((tm, d), lambda i: (i, 0))]
    if not final:
        out_shape.insert(0, jax.ShapeDtypeStruct((n, d), F32))
        out_specs.insert(0, pl.BlockSpec((tm, d), lambda i: (i, 0)))
    scratch = source.scratch(tm)
    outs = pl.pallas_call(
        functools.partial(_out_proj_kernel, n_src=len(source.inputs), n_scratch=len(scratch),
                          make_x=source.make_x(tm), emit_h=not final),
        grid=(n // tm,),
        in_specs=[
            *source.specs(tm),
            pl.BlockSpec((k, d), lambda i: (0, 0), pipeline_mode=pl.Buffered(1)),
            pl.BlockSpec((tm, d), lambda i: (i, 0)),
            pl.BlockSpec((1, d), lambda i: (0, 0)),
        ],
        out_specs=out_specs,
        out_shape=out_shape,
        scratch_shapes=scratch,
        compiler_params=_params(("arbitrary",), pipelined_bytes(tm), resident_bytes(tm)),
        name=source.name,
    )(*source.inputs, w, h, g.reshape(1, d))
    return (None, outs[0]) if final else (outs[0], outs[1])


def _ffn_up_kernel(a_ref, wg_ref, wu_ref, cw_ref, cb_ref, t_ref, carry_ref, wg_bf_ref, wu_bf_ref, *,
                   tm, seq):
    i = pl.program_id(1)

    @pl.when(i == 0)
    def _():
        wg_bf_ref[...] = wg_ref[...].astype(BF16)
        wu_bf_ref[...] = wu_ref[...].astype(BF16)

    @pl.when((i * tm) % seq == 0)
    def _():
        carry_ref[...] = jnp.zeros_like(carry_ref)

    a = a_ref[...]
    g = jnp.dot(a, wg_bf_ref[...], preferred_element_type=F32)
    up = jnp.dot(a, wu_bf_ref[...], preferred_element_type=F32)
    prev = carry_ref[...]
    cw = cw_ref[...]
    conv = (cw[0:1, :] * _shift_rows(g, prev, 2) + cw[1:2, :] * _shift_rows(g, prev, 1)
            + cw[2:3, :] * g + cb_ref[...])
    t_ref[...] = (conv * jax.nn.sigmoid(conv) * up).astype(t_ref.dtype)
    carry_ref[...] = g[tm - V7X_SUBLANES:, :]


def _ffn_up(a, w_gate, w_up, layer, conv_w, conv_b, *, seq):
    n, d = a.shape
    f = w_gate.shape[2]
    tm = _row_tile(seq, 1024)
    tn = 512
    assert f % tn == 0
    pipelined = (_nbytes((tm, d), BF16) + 2 * _nbytes((d, tn), F32) + _nbytes((tm, tn), BF16)
                 + 2 * _nbytes((tm, tn), F32))
    w_spec = pl.BlockSpec((None, d, tn), lambda j, i: (layer, 0, j))
    return pl.pallas_call(
        functools.partial(_ffn_up_kernel, tm=tm, seq=seq),
        grid=(f // tn, n // tm),
        in_specs=[
            pl.BlockSpec((tm, d), lambda j, i: (i, 0)),
            w_spec,
            w_spec,
            pl.BlockSpec((CONV_WIDTH, tn), lambda j, i: (0, j)),
            pl.BlockSpec((1, tn), lambda j, i: (0, j)),
        ],
        out_specs=pl.BlockSpec((tm, tn), lambda j, i: (i, j)),
        out_shape=jax.ShapeDtypeStruct((n, f), BF16),
        scratch_shapes=[pltpu.VMEM((V7X_SUBLANES, tn), F32), pltpu.VMEM((d, tn), BF16),
                        pltpu.VMEM((d, tn), BF16)],
        compiler_params=_params(("arbitrary", "arbitrary"), pipelined, 2 * _nbytes((d, tn), BF16)),
        name="ffn_up",
    )(a, w_gate, w_up, conv_w, conv_b.reshape(1, f))


def _sc_in_kernel(a_ref, wb_ref, wc_ref, wx_ref, gb_ref, u_ref):
    a = a_ref[...]
    gb_ref[...] = jnp.dot(a, wb_ref[...], preferred_element_type=F32)
    gc = jnp.dot(a, wc_ref[...], preferred_element_type=F32)
    xi = jnp.dot(a, wx_ref[...], preferred_element_type=F32)
    u_ref[...] = gc * xi


def _sc_in(a, w_in):
    n, d = a.shape
    tm = _row_tile(n, 1024)
    tn = 512
    nj = d // tn
    pipelined = _nbytes((tm, d), BF16) + 3 * _nbytes((d, tn), BF16) + 3 * _nbytes((tm, tn), F32)
    return pl.pallas_call(
        _sc_in_kernel,
        grid=(nj, n // tm),
        in_specs=[
            pl.BlockSpec((tm, d), lambda j, i: (i, 0)),
            pl.BlockSpec((d, tn), lambda j, i: (0, j)),
            pl.BlockSpec((d, tn), lambda j, i: (0, j + nj)),
            pl.BlockSpec((d, tn), lambda j, i: (0, j + 2 * nj)),
        ],
        out_specs=[pl.BlockSpec((tm, tn), lambda j, i: (i, j))] * 2,
        out_shape=[jax.ShapeDtypeStruct((n, d), F32)] * 2,
        compiler_params=_params(("arbitrary", "arbitrary"), pipelined),
        name="sc_in",
    )(a, w_in, w_in, w_in)


def _sc_gate(u_ref, halo_ref, gb_ref, cw_ref, *, tm, seq):
    i = pl.program_id(0)
    u = u_ref[...]
    prev = jnp.where((i * tm) % seq == 0, 0.0, halo_ref[...])
    cw = cw_ref[...]
    conv = cw[0:1, :] * _shift_rows(u, prev, 2) + cw[1:2, :] * _shift_rows(u, prev, 1) + cw[2:3, :] * u
    return (gb_ref[...] * conv).astype(BF16)


def _sc_source(u, gb, conv_w, *, seq):
    d = u.shape[1]

    def specs(tm):
        assert seq % tm == 0
        hb = tm // V7X_SUBLANES
        return [
            pl.BlockSpec((tm, d), lambda i: (i, 0)),
            pl.BlockSpec((V7X_SUBLANES, d), lambda i: (jnp.maximum(i * hb - 1, 0), 0)),
            pl.BlockSpec((tm, d), lambda i: (i, 0)),
            pl.BlockSpec((CONV_WIDTH, d), lambda i: (0, 0)),
        ]

    return _Source(lambda tm: functools.partial(_sc_gate, tm=tm, seq=seq), [u, u, gb, conv_w], specs,
                   2 * d * 4, lambda tm: [], lambda tm: 2 * _nbytes((tm, d), F32), "sc_out")


def _sg_in_kernel(a_ref, wu_ref, wv_ref, u_ref, v_ref):
    a = a_ref[...]
    u_ref[...] = jax.nn.gelu(jnp.dot(a, wu_ref[...], preferred_element_type=F32))
    v_ref[...] = jax.nn.gelu(jnp.dot(a, wv_ref[...], preferred_element_type=F32))


def _sg_in(a, w_in):
    n, d = a.shape
    width = w_in.shape[1] // 2
    tm = _row_tile(n, 1024)
    tn = 512
    nj = width // tn
    pipelined = _nbytes((tm, d), BF16) + 2 * _nbytes((d, tn), BF16) + 2 * _nbytes((tm, tn), F32)
    return pl.pallas_call(
        _sg_in_kernel,
        grid=(nj, n // tm),
        in_specs=[
            pl.BlockSpec((tm, d), lambda j, i: (i, 0)),
            pl.BlockSpec((d, tn), lambda j, i: (0, j)),
            pl.BlockSpec((d, tn), lambda j, i: (0, j + nj)),
        ],
        out_specs=[pl.BlockSpec((tm, tn), lambda j, i: (i, j))] * 2,
        out_shape=[jax.ShapeDtypeStruct((n, width), F32)] * 2,
        compiler_params=_params(("arbitrary", "arbitrary"), pipelined),
        name="sg_in",
    )(a, w_in, w_in)


def _sg_gate(u_ref, v_ref, lng_ref, lnb_ref, ws_ref, bs_ref, vn_ref, y_ref, *, tm):
    v = v_ref[...]
    mu = jnp.mean(v, axis=-1, keepdims=True)
    vc = v - mu
    var = jnp.mean(vc * vc, axis=-1, keepdims=True)
    vn_ref[...] = (vc * lax.rsqrt(var + EPS) * lng_ref[...] + lnb_ref[...]).astype(vn_ref.dtype)
    t = SGU_CHUNK
    nchunk = tm // t
    row = lax.broadcasted_iota(jnp.int32, (t, t), 0)
    col = lax.broadcasted_iota(jnp.int32, (t, t), 1)
    bs = bs_ref[...]
    for g in range(SGU_GROUPS):
        cols = slice(g * t, (g + 1) * t)
        w = jnp.where(row >= col, ws_ref[g], 0.0).astype(BF16)
        rhs = jnp.concatenate([vn_ref[c * t:(c + 1) * t, cols] for c in range(nchunk)], axis=1)
        mixed = jnp.dot(w, rhs, preferred_element_type=F32) + bs[:, g:g + 1]
        for c in range(nchunk):
            rows = slice(c * t, (c + 1) * t)
            y_ref[rows, cols] = (u_ref[rows, cols] * mixed[:, c * t:(c + 1) * t]).astype(y_ref.dtype)
    return y_ref[...]


def _sg_source(u, v, ln_g, ln_b, w_s, b_s):
    width = u.shape[1]

    def specs(tm):
        assert tm % SGU_CHUNK == 0
        return [
            pl.BlockSpec((tm, width), lambda i: (i, 0)),
            pl.BlockSpec((tm, width), lambda i: (i, 0)),
            pl.BlockSpec((1, width), lambda i: (0, 0)),
            pl.BlockSpec((1, width), lambda i: (0, 0)),
            pl.BlockSpec(w_s.shape, lambda i: (0, 0, 0)),
            pl.BlockSpec((SGU_CHUNK, SGU_GROUPS), lambda i: (0, 0)),
        ]

    return _Source(lambda tm: functools.partial(_sg_gate, tm=tm),
                   [u, v, ln_g.reshape(1, width), ln_b.reshape(1, width), w_s, b_s.T], specs,
                   2 * width * 4, lambda tm: [pltpu.VMEM((tm, width), BF16)] * 2,
                   lambda tm: 2 * _nbytes(w_s.shape, F32) + 2 * _nbytes((tm, width), BF16), "sg_out")


DIFF_TQ = 512
DIFF_TK = 512
DIFF_HEADS_PER_STEP = 2
DIFF_CHUNK = 256
SB_TQ = 256
SB_TK = 256
SB_HEADS_PER_STEP = 4
SB_DEAD_LOG2 = -160.0
LOG2E = 1.4426950408889634


def _rope_table_kernel(pos_ref, invf_ref, sign_ref, cos_ref, sin_ref):
    ang = pos_ref[...] * invf_ref[...]
    cos_ref[...] = jnp.cos(ang)
    sin_ref[...] = jnp.sin(ang) * sign_ref[...]


def _rope_tables(positions, head_dim):
    n = positions.size
    half = head_dim // 2
    inv_freq = 1.0 / (ROPE_THETA ** (jnp.arange(0, head_dim, 2, dtype=F32) / head_dim))
    lane = jnp.arange(V7X_LANES)
    invf = inv_freq[lane % half].reshape(1, V7X_LANES)
    sign = jnp.where(lane % head_dim < half, -1.0, 1.0).astype(F32).reshape(1, V7X_LANES)
    tm = _row_tile(n, 1024)
    return pl.pallas_call(
        _rope_table_kernel,
        grid=(n // tm,),
        in_specs=[
            pl.BlockSpec((tm, 1), lambda i: (i, 0)),
            pl.BlockSpec((1, V7X_LANES), lambda i: (0, 0)),
            pl.BlockSpec((1, V7X_LANES), lambda i: (0, 0)),
        ],
        out_specs=[pl.BlockSpec((tm, V7X_LANES), lambda i: (i, 0))] * 2,
        out_shape=[jax.ShapeDtypeStruct((n, V7X_LANES), F32)] * 2,
        compiler_params=_params(("arbitrary",), 3 * _nbytes((tm, V7X_LANES), F32)),
        name="rope_tables",
    )(positions.astype(F32).reshape(n, 1), invf, sign)


def _rotate_half(x, cos, sin, first_half):
    partner = jnp.where(first_half, pltpu.roll(x, 96, 1), pltpu.roll(x, 32, 1))
    return x * cos + partner * sin


def _qkv_kernel(a_ref, wq_ref, wk_ref, wv_ref, *refs, rope, q_scale, tk):
    if rope:
        cos_ref, sin_ref, q_ref, k_ref, vt_ref = refs
    else:
        q_ref, k_ref, vt_ref = refs
    a = a_ref[...]
    q = jnp.dot(a, wq_ref[...], preferred_element_type=F32)
    k = jnp.dot(a, wk_ref[...], preferred_element_type=F32)
    v = jnp.dot(a, wv_ref[...], preferred_element_type=F32)
    tm, tn = q.shape
    if rope:
        cos = cos_ref[...]
        sin = sin_ref[...]
        lane = lax.broadcasted_iota(jnp.int32, cos.shape, 1)
        first_half = (lane % 64) < 32
        for c in range(tn // V7X_LANES):
            cols = slice(c * V7X_LANES, (c + 1) * V7X_LANES)
            q_ref[:, cols] = (_rotate_half(q[:, cols], cos, sin, first_half) * q_scale).astype(q_ref.dtype)
            k_ref[:, cols] = _rotate_half(k[:, cols], cos, sin, first_half).astype(k_ref.dtype)
    else:
        q_ref[...] = q.astype(q_ref.dtype)
        k_ref[...] = k.astype(k_ref.dtype)
    for hh in range(tn // HEAD_WIDTH):
        for kk in range(tm // tk):
            blk = v[kk * tk:(kk + 1) * tk, hh * HEAD_WIDTH:(hh + 1) * HEAD_WIDTH]
            vt_ref[hh, kk] = blk.T.astype(vt_ref.dtype)


def _qkv(a, w_qkv, *, tk, rope_tables=None, q_scale=1.0):
    n, d = a.shape
    width = w_qkv.shape[1] // 3
    tm = _row_tile(n, 1024)
    tn = 512
    nj = width // tn
    rope = rope_tables is not None
    in_specs = [
        pl.BlockSpec((tm, d), lambda j, i: (i, 0)),
        pl.BlockSpec((d, tn), lambda j, i: (0, j)),
        pl.BlockSpec((d, tn), lambda j, i: (0, j + nj)),
        pl.BlockSpec((d, tn), lambda j, i: (0, j + 2 * nj)),
    ]
    args = [a, w_qkv, w_qkv, w_qkv]
    if rope:
        in_specs += [pl.BlockSpec((tm, V7X_LANES), lambda j, i: (i, 0))] * 2
        args += list(rope_tables)
    heads = width // HEAD_WIDTH
    pipelined = (_nbytes((tm, d), BF16) + 3 * _nbytes((d, tn), BF16) + 3 * _nbytes((tm, tn), BF16)
                 + 3 * _nbytes((tm, tn), F32) + 2 * _nbytes((tm, V7X_LANES), F32))
    return pl.pallas_call(
        functools.partial(_qkv_kernel, rope=rope, q_scale=q_scale, tk=tk),
        grid=(nj, n // tm),
        in_specs=in_specs,
        out_specs=[
            pl.BlockSpec((tm, tn), lambda j, i: (i, j)),
            pl.BlockSpec((tm, tn), lambda j, i: (i, j)),
            pl.BlockSpec((tn // HEAD_WIDTH, tm // tk, HEAD_WIDTH, tk), lambda j, i: (j, i, 0, 0)),
        ],
        out_shape=[
            jax.ShapeDtypeStruct((n, width), BF16),
            jax.ShapeDtypeStruct((n, width), BF16),
            jax.ShapeDtypeStruct((heads, n // tk, HEAD_WIDTH, tk), BF16),
        ],
        compiler_params=_params(("arbitrary", "arbitrary"), pipelined),
        name="qkv_rope" if rope else "qkv",
    )(*args)


def _attn_specs(seq, tq, tk, heads):
    nq = seq // tq
    nk = seq // tk
    w = heads * HEAD_WIDTH
    q_spec = pl.BlockSpec((tq, w), lambda b, h, qi: (b * nq + qi, h))
    k_specs = [pl.BlockSpec((seq, HEAD_WIDTH), functools.partial(lambda b, h, qi, hh: (b, h * heads + hh), hh=hh))
               for hh in range(heads)]
    vt_spec = pl.BlockSpec((heads, nk, HEAD_WIDTH, tk), lambda b, h, qi: (h, b, 0, 0))
    o_spec = pl.BlockSpec((tq, w), lambda b, h, qi: (b * nq + qi, h))
    return q_spec, k_specs, vt_spec, o_spec


def _attn_params(seq, tq, tk, heads, score_cols):
    w = heads * HEAD_WIDTH
    pipelined = 2 * _nbytes((tq, w), BF16) + 2 * _nbytes((seq, w), BF16)
    scratch = (_nbytes((2, heads, tk, score_cols), F32) + _nbytes((2, heads, tk, score_cols), BF16)
               + _nbytes((heads, HEAD_WIDTH, score_cols), F32))
    return _params(("arbitrary", "arbitrary", "arbitrary"), pipelined, scratch)


def _head_cols(hh):
    return slice(hh * HEAD_WIDTH, (hh + 1) * HEAD_WIDTH)


def _diff_attn_kernel(q_ref, *refs, tq, tk, heads, lambda_init):
    k_refs = refs[:heads]
    (vt_ref, lq1_ref, lk1_ref, lq2_ref, lk2_ref, g_ref, o_ref,
     qq_ref, s_ref, p_ref, alpha_ref, m_ref, l_ref, acc_ref) = refs[heads:]
    qi = pl.program_id(2)
    feature = lax.broadcasted_iota(jnp.int32, (HEAD_WIDTH, tq), 0)
    for hh in range(heads):
        qt = q_ref[:, _head_cols(hh)].astype(F32).T
        qq_ref[hh, :, :tq] = jnp.where(feature < 64, qt, 0.0).astype(BF16)
        qq_ref[hh, :, tq:] = jnp.where(feature >= 64, qt, 0.0).astype(BF16)
        m_ref[hh] = jnp.full(m_ref.shape[1:], -jnp.inf, F32)
        l_ref[hh] = jnp.zeros(l_ref.shape[1:], F32)
        acc_ref[hh] = jnp.zeros(acc_ref.shape[1:], F32)

    chunks = [(hh, slice(c, c + DIFF_CHUNK)) for hh in range(heads) for c in range(0, 2 * tq, DIFF_CHUNK)]

    def scores_to(slot, kb, hh, cols):
        rows = pl.ds(pl.multiple_of(kb * tk, tk), tk)
        s_ref[slot, hh, cols.start // DIFF_CHUNK] = jnp.dot(
            k_refs[hh][rows, :], qq_ref[hh, :, cols], preferred_element_type=F32)

    def accumulate(slot, kb, hh, cols):
        ci = cols.start // DIFF_CHUNK
        acc_ref[hh, ci] = alpha_ref[slot, hh, :, cols] * acc_ref[hh, ci] + jnp.dot(
            vt_ref[hh, kb], p_ref[slot, hh, cols.start // DIFF_CHUNK], preferred_element_type=F32)

    def softmax_to(slot, kb, hh, cols, diagonal):
        s = s_ref[slot, hh, cols.start // DIFF_CHUNK]
        if diagonal:
            key = kb * tk + lax.broadcasted_iota(jnp.int32, s.shape, 0)
            qry = qi * tq + cols.start % tq + lax.broadcasted_iota(jnp.int32, s.shape, 1)
            s = jnp.where(key <= qry, s, -jnp.inf)
        m = m_ref[hh, :, cols]
        m_new = jnp.maximum(m, jnp.max(s, axis=0, keepdims=True))
        alpha = jnp.exp2(m - m_new)
        p = jnp.exp2(s - m_new)
        m_ref[hh, :, cols] = m_new
        l_ref[hh, :, cols] = alpha * l_ref[hh, :, cols] + jnp.sum(p, axis=0, keepdims=True)
        alpha_ref[slot, hh, :, cols] = alpha
        p_ref[slot, hh, cols.start // DIFF_CHUNK] = p.astype(BF16)

    def step(slot, kb, prev_kb, diagonal):
        other = 1 - slot
        for hh, cols in chunks:
            if not diagonal:
                accumulate(other, prev_kb, hh, cols)
            scores_to(other, jnp.maximum(kb - 1, 0), hh, cols)
            softmax_to(slot, kb, hh, cols, diagonal)

    nfull = (qi * tq) // tk
    for hh, cols in chunks:
        scores_to(0, nfull, hh, cols)
    step(0, nfull, None, True)

    def body(t, _):
        kb = nfull - 1 - t

        @pl.when(t % 2 == 0)
        def _():
            step(1, kb, kb + 1, False)

        @pl.when(t % 2 == 1)
        def _():
            step(0, kb, kb + 1, False)

        return 0

    lax.fori_loop(0, nfull, body, 0)

    @pl.when(nfull % 2 == 0)
    def _():
        for hh, cols in chunks:
            accumulate(0, 0, hh, cols)

    @pl.when(nfull % 2 == 1)
    def _():
        for hh, cols in chunks:
            accumulate(1, 0, hh, cols)

    lam = (jnp.exp(jnp.sum(lq1_ref[...] * lk1_ref[...], axis=-1, keepdims=True))
           - jnp.exp(jnp.sum(lq2_ref[...] * lk2_ref[...], axis=-1, keepdims=True)) + lambda_init)
    for hh in range(heads):
        for c in range(0, tq, DIFF_CHUNK):
            first, second = slice(c, c + DIFF_CHUNK), slice(tq + c, tq + c + DIFF_CHUNK)
            o = (acc_ref[hh, first.start // DIFF_CHUNK] / l_ref[hh, :, first]
                 - lam * (acc_ref[hh, second.start // DIFF_CHUNK] / l_ref[hh, :, second]))
            ms = jnp.mean(o * o, axis=0, keepdims=True)
            o = o * lax.rsqrt(ms + EPS) * g_ref[...] * (1.0 - lambda_init)
            o_ref[first, _head_cols(hh)] = o.T.astype(o_ref.dtype)


def _diff_attn(q, k, vt, lq1, lk1, lq2, lk2, subln_g, *, batch, seq, lambda_init):
    n, width = q.shape
    tq, tk, heads = DIFF_TQ, DIFF_TK, DIFF_HEADS_PER_STEP
    q_spec, k_specs, vt_spec, o_spec = _attn_specs(seq, tq, tk, heads)
    dd = lq1.shape[0]
    small = pl.BlockSpec((1, dd), lambda b, h, qi: (0, 0))
    return pl.pallas_call(
        functools.partial(_diff_attn_kernel, tq=tq, tk=tk, heads=heads, lambda_init=lambda_init),
        grid=(batch, width // (heads * HEAD_WIDTH), seq // tq),
        in_specs=[q_spec, *k_specs, vt_spec, small, small, small, small,
                  pl.BlockSpec((HEAD_WIDTH, 1), lambda b, h, qi: (0, 0))],
        out_specs=o_spec,
        out_shape=jax.ShapeDtypeStruct((n, width), BF16),
        scratch_shapes=[
            pltpu.VMEM((heads, HEAD_WIDTH, 2 * tq), BF16),
            pltpu.VMEM((2, heads, 2 * tq // DIFF_CHUNK, tk, DIFF_CHUNK), F32),
            pltpu.VMEM((2, heads, 2 * tq // DIFF_CHUNK, tk, DIFF_CHUNK), BF16),
            pltpu.VMEM((2, heads, 1, 2 * tq), F32),
            pltpu.VMEM((heads, 1, 2 * tq), F32),
            pltpu.VMEM((heads, 1, 2 * tq), F32),
            pltpu.VMEM((heads, 2 * tq // DIFF_CHUNK, HEAD_WIDTH, DIFF_CHUNK), F32),
        ],
        compiler_params=_attn_params(seq, tq, tk, heads, 2 * tq),
        name="diff_attn",
    )(q, *[k] * heads, vt, lq1.reshape(1, dd), lk1.reshape(1, dd), lq2.reshape(1, dd), lk2.reshape(1, dd),
      subln_g.reshape(HEAD_WIDTH, 1))


def _sb_attn_kernel(q_ref, *refs, tq, tk, heads, scale):
    k_refs = refs[:heads]
    vt_ref, o_ref, qt_ref, later_ref, z_ref, tail_ref, acc_ref = refs[heads:]
    qi = pl.program_id(2)
    row = lax.broadcasted_iota(jnp.int32, (tk, tk), 0)
    col = lax.broadcasted_iota(jnp.int32, (tk, tk), 1)
    later_ref[...] = jnp.where(col > row, -1.0, 0.0).astype(BF16)
    for hh in range(heads):
        qt_ref[hh] = q_ref[:, _head_cols(hh)].astype(F32).T.astype(BF16)
    tail_ref[...] = jnp.zeros_like(tail_ref)
    acc_ref[...] = jnp.zeros_like(acc_ref)

    def scores_to(slot, kb):
        rows = pl.ds(pl.multiple_of(kb * tk, tk), tk)
        for hh in range(heads):
            z_ref[slot, hh] = jnp.dot(k_refs[hh][rows, :], qt_ref[hh],
                                      preferred_element_type=F32) * (scale * LOG2E)

    def visit(blocks):
        neg_later = later_ref[...]
        his, los, colsums = {}, {}, {}
        for slot, _, mask in blocks:
            for hh in range(heads):
                z = z_ref[slot, hh]
                softplus = jnp.maximum(z, 0.0) + jnp.log2(1.0 + jnp.exp2(-jnp.abs(z)))
                z_ref[slot, hh] = z - softplus
                if mask is not None:
                    softplus = jnp.where(mask, softplus, 0.0)
                hi = softplus.astype(BF16)
                his[slot, hh] = hi
                los[slot, hh] = (softplus - hi.astype(F32)).astype(BF16)
                colsums[slot, hh] = jnp.sum(softplus, axis=0, keepdims=True)
        survives = {key: jnp.dot(neg_later, his[key], preferred_element_type=F32)
                    + jnp.dot(neg_later, los[key], preferred_element_type=F32) for key in his}
        weights = {}
        for hh in range(heads):
            tail = tail_ref[hh]
            for slot, _, mask in blocks:
                a = jnp.exp2(z_ref[slot, hh] + survives[slot, hh] + tail)
                if mask is not None:
                    a = jnp.where(mask, a, 0.0)
                weights[slot, hh] = a.astype(BF16)
                tail = tail - colsums[slot, hh]
            tail_ref[hh] = tail
        for slot, kb, _ in blocks:
            for hh in range(heads):
                acc_ref[hh] += jnp.dot(vt_ref[hh, kb], weights[slot, hh], preferred_element_type=F32)

    key = qi * tk + lax.broadcasted_iota(jnp.int32, (tk, tq), 0)
    qry = qi * tq + lax.broadcasted_iota(jnp.int32, (tk, tq), 1)
    before = jnp.maximum(qi - 1, 0)
    scores_to(0, qi)
    scores_to(1, before)
    visit([(0, qi, key < qry), (1, before, jnp.broadcast_to(qi > 0, (tk, tq)))])

    def any_live():
        return jnp.max(tail_ref[...]) > SB_DEAD_LOG2

    def body(c):
        kb = qi - 2 - c[0]
        scores_to(0, kb)
        visit([(0, kb, None)])
        return c[0] + 1, any_live()

    lax.while_loop(lambda c: jnp.logical_and(c[0] < qi - 1, c[1]), body, (jnp.int32(0), any_live()))
    for hh in range(heads):
        o_ref[:, _head_cols(hh)] = acc_ref[hh].T.astype(o_ref.dtype)


def _sb_attn(q, k, vt, *, batch, seq, scale):
    n, width = q.shape
    tq, tk, heads = SB_TQ, SB_TK, SB_HEADS_PER_STEP
    assert tq == tk
    q_spec, k_specs, vt_spec, o_spec = _attn_specs(seq, tq, tk, heads)
    return pl.pallas_call(
        functools.partial(_sb_attn_kernel, tq=tq, tk=tk, heads=heads, scale=scale),
        grid=(batch, width // (heads * HEAD_WIDTH), seq // tq),
        in_specs=[q_spec, *k_specs, vt_spec],
        out_specs=o_spec,
        out_shape=jax.ShapeDtypeStruct((n, width), BF16),
        scratch_shapes=[
            pltpu.VMEM((heads, HEAD_WIDTH, tq), BF16),
            pltpu.VMEM((tk, tk), BF16),
            pltpu.VMEM((2, heads, tk, tq), F32),
            pltpu.VMEM((heads, 1, tq), F32),
            pltpu.VMEM((heads, HEAD_WIDTH, tq), F32),
        ],
        compiler_params=_attn_params(seq, tq, tk, heads, tq),
        name="sb_attn",
    )(q, *[k] * heads, vt)


def kernel(x, positions, norm_mix_g, norm_ffn_g, norm_final_g, sc_w_in, sc_conv_w, sc_w_out, sg_w_in, sg_ln_g, sg_ln_b, sg_w_s, sg_b_s, sg_w_out, da_w_qkv, da_lambda_q1, da_lambda_k1, da_lambda_q2, da_lambda_k2, da_subln_g, da_w_out, sb_w_qkv, sb_w_out, ffn_w_gate, ffn_w_up, ffn_conv_w, ffn_conv_b, ffn_w_down):
    batch, seq, d = x.shape
    depth = norm_mix_g.shape[0]
    n = batch * seq
    assert seq % DIFF_TK == 0 and seq % DIFF_TQ == 0 and seq % SB_TQ == 0 and d % HEAD_WIDTH == 0

    def bf(w):
        return w.astype(BF16)

    h = x.reshape(n, d)
    a = _rmsnorm(h, norm_mix_g[0])
    for layer in range(depth):
        mixer, j = layer % 4, layer // 4
        if mixer == 0:
            gb, u = _sc_in(a, bf(sc_w_in[j]))
            y = _sc_source(u, gb, sc_conv_w[j], seq=seq)
            w_out = sc_w_out[j]
        elif mixer == 1:
            u, v = _sg_in(a, bf(sg_w_in[j]))
            y = _sg_source(u, v, sg_ln_g[j], sg_ln_b[j], sg_w_s[j], sg_b_s[j])
            w_out = sg_w_out[j]
        elif mixer == 2:
            head_dim = d // DIFF_HEADS // 2
            lambda_init = 0.8 - 0.6 * math.exp(-0.3 * layer)
            tables = _rope_tables(positions, head_dim)
            q, k, vt = _qkv(a, bf(da_w_qkv[j]), tk=DIFF_TK, rope_tables=tables,
                            q_scale=head_dim ** -0.5 * LOG2E)
            y = _diff_attn(q, k, vt, da_lambda_q1[j], da_lambda_k1[j], da_lambda_q2[j], da_lambda_k2[j],
                           da_subln_g[j], batch=batch, seq=seq, lambda_init=lambda_init)
            w_out = da_w_out[j]
        else:
            q, k, vt = _qkv(a, bf(sb_w_qkv[j]), tk=SB_TK)
            y = _sb_attn(q, k, vt, batch=batch, seq=seq, scale=(d // SB_HEADS) ** -0.5)
            w_out = sb_w_out[j]
        h, a = _out_proj(y, bf(w_out), h, norm_ffn_g[layer])
        t = _ffn_up(a, ffn_w_gate, ffn_w_up, layer, ffn_conv_w[layer], ffn_conv_b[layer], seq=seq)
        last = layer == depth - 1
        h, a = _out_proj(t, bf(ffn_w_down[layer]), h, norm_final_g if last else norm_mix_g[layer + 1], final=last)
    return a.reshape(batch, seq, d)
```

```python
import functools
import math
from typing import Callable, NamedTuple, Sequence

import jax
import jax.numpy as jnp
from jax import lax
from jax.experimental import pallas as pl
from jax.experimental.pallas import tpu as pltpu

F32 = jnp.float32
BF16 = jnp.bfloat16

EPS = 1e-6
ROPE_THETA = 10000.0
CONV_WIDTH = 3
SGU_GROUPS = 16
SGU_CHUNK = 128
DIFF_HEADS = 16
SB_HEADS = 16
HEAD_WIDTH = 128

V7X_LANES = 128
V7X_SUBLANES = 8
V7X_VMEM_BYTES = 64 * 1024 * 1024
INTERNAL_SCRATCH_BYTES = 12 * 1024 * 1024


def _nbytes(shape, dtype):
    return math.prod(shape) * jnp.dtype(dtype).itemsize


def _params(semantics, pipelined_bytes, resident_bytes=0):
    limit = 2 * pipelined_bytes + resident_bytes + INTERNAL_SCRATCH_BYTES
    limit = min(limit, V7X_VMEM_BYTES - 4 * 1024 * 1024)
    return pltpu.CompilerParams(dimension_semantics=semantics, vmem_limit_bytes=int(limit))


def _row_tile(n, want):
    t = min(n, want)
    assert n % t == 0, (n, t)
    return t


def _rms(x, g):
    ms = jnp.mean(x * x, axis=-1, keepdims=True)
    return x * lax.rsqrt(ms + EPS) * g


def _shift_rows(x, prev, k):
    row = lax.broadcasted_iota(jnp.int32, x.shape, 0)
    out = pltpu.roll(x, k, 0)
    for r in range(k):
        out = jnp.where(row == r, prev[V7X_SUBLANES - k + r:V7X_SUBLANES - k + r + 1, :], out)
    return out


def _rmsnorm_kernel(x_ref, g_ref, o_ref):
    o_ref[...] = _rms(x_ref[...], g_ref[...]).astype(o_ref.dtype)


def _rmsnorm(h, g):
    n, d = h.shape
    tm = _row_tile(n, 512)
    return pl.pallas_call(
        _rmsnorm_kernel,
        grid=(n // tm,),
        in_specs=[pl.BlockSpec((tm, d), lambda i: (i, 0)), pl.BlockSpec((1, d), lambda i: (0, 0))],
        out_specs=pl.BlockSpec((tm, d), lambda i: (i, 0)),
        out_shape=jax.ShapeDtypeStruct((n, d), BF16),
        compiler_params=_params(("arbitrary",), _nbytes((tm, d), F32) + _nbytes((tm, d), BF16)),
        name="rmsnorm",
    )(h, g.reshape(1, d))


class _Source(NamedTuple):
    make_x: Callable
    inputs: Sequence
    specs: Callable
    row_bytes: int
    scratch: Callable
    resident_bytes: Callable
    name: str


def _plain_source(x):
    k = x.shape[1]
    return _Source(lambda tm: lambda x_ref: x_ref[...], [x],
                   lambda tm: [pl.BlockSpec((tm, k), lambda i: (i, 0))],
                   k * 2, lambda tm: [], lambda tm: 0, "out_proj")


def _out_proj_kernel(*refs, n_src, n_scratch, make_x, emit_h):
    src = refs[:n_src]
    w_ref, h_ref, g_ref = refs[n_src:n_src + 3]
    out_refs = refs[n_src + 3:len(refs) - n_scratch]
    scratch = refs[len(refs) - n_scratch:]
    hn = h_ref[...] + jnp.dot(make_x(*src, *scratch), w_ref[...], preferred_element_type=F32)
    if emit_h:
        out_refs[0][...] = hn
    an_ref = out_refs[-1]
    an_ref[...] = _rms(hn, g_ref[...]).astype(an_ref.dtype)


def _out_proj(source, w, layer, h, g, *, final=False):
    if not isinstance(source, _Source):
        source = _plain_source(source)
    n, d = h.shape
    k = w.shape[1]
    an_dtype = F32 if final else BF16

    def pipelined_bytes(tm):
        return tm * source.row_bytes + 2 * _nbytes((tm, d), F32) + _nbytes((tm, d), an_dtype)

    def resident_bytes(tm):
        return _nbytes((k, d), BF16) + source.resident_bytes(tm)

    budget = V7X_VMEM_BYTES - 4 * 1024 * 1024 - INTERNAL_SCRATCH_BYTES
    tm = next(t for t in (512, 256, 128)
              if n % t == 0 and 2 * pipelined_bytes(t) + resident_bytes(t) <= budget)
    out_shape = [jax.ShapeDtypeStruct((n, d), an_dtype)]
    out_specs = [pl.BlockSpec((tm, d), lambda i: (i, 0))]
    if not final:
        out_shape.insert(0, jax.ShapeDtypeStruct((n, d), F32))
        out_specs.insert(0, pl.BlockSpec((tm, d), lambda i: (i, 0)))
    scratch = source.scratch(tm)
    outs = pl.pallas_call(
        functools.partial(_out_proj_kernel, n_src=len(source.inputs), n_scratch=len(scratch),
                          make_x=source.make_x(tm), emit_h=not final),
        grid=(n // tm,),
        in_specs=[
            *source.specs(tm),
            pl.BlockSpec((None, k, d), lambda i: (layer, 0, 0), pipeline_mode=pl.Buffered(1)),
            pl.BlockSpec((tm, d), lambda i: (i, 0)),
            pl.BlockSpec((1, d), lambda i: (0, 0)),
        ],
        out_specs=out_specs,
        out_shape=out_shape,
        scratch_shapes=scratch,
        compiler_params=_params(("arbitrary",), pipelined_bytes(tm), resident_bytes(tm)),
        name=source.name,
    )(*source.inputs, w, h, g.reshape(1, d))
    return (None, outs[0]) if final else (outs[0], outs[1])


def _ffn_up_kernel(a_ref, wg_ref, wu_ref, cw_ref, cb_ref, t_ref, carry_ref, wg_bf_ref, wu_bf_ref, *,
                   tm, seq):
    i = pl.program_id(1)

    @pl.when(i == 0)
    def _():
        wg_bf_ref[...] = wg_ref[...].astype(BF16)
        wu_bf_ref[...] = wu_ref[...].astype(BF16)

    @pl.when((i * tm) % seq == 0)
    def _():
        carry_ref[...] = jnp.zeros_like(carry_ref)

    a = a_ref[...]
    g = jnp.dot(a, wg_bf_ref[...], preferred_element_type=F32)
    up = jnp.dot(a, wu_bf_ref[...], preferred_element_type=F32)
    prev = carry_ref[...]
    cw = cw_ref[...]
    conv = (cw[0:1, :] * _shift_rows(g, prev, 2) + cw[1:2, :] * _shift_rows(g, prev, 1)
            + cw[2:3, :] * g + cb_ref[...])
    t_ref[...] = (conv * jax.nn.sigmoid(conv) * up).astype(t_ref.dtype)
    carry_ref[...] = g[tm - V7X_SUBLANES:, :]


def _ffn_up(a, w_gate, w_up, layer, conv_w, conv_b, *, seq):
    n, d = a.shape
    f = w_gate.shape[2]
    tm = _row_tile(seq, 1024)
    tn = 512
    assert f % tn == 0
    pipelined = (_nbytes((tm, d), BF16) + 2 * _nbytes((d, tn), F32) + _nbytes((tm, tn), BF16)
                 + 2 * _nbytes((tm, tn), F32))
    w_spec = pl.BlockSpec((None, d, tn), lambda j, i: (layer, 0, j))
    return pl.pallas_call(
        functools.partial(_ffn_up_kernel, tm=tm, seq=seq),
        grid=(f // tn, n // tm),
        in_specs=[
            pl.BlockSpec((tm, d), lambda j, i: (i, 0)),
            w_spec,
            w_spec,
            pl.BlockSpec((CONV_WIDTH, tn), lambda j, i: (0, j)),
            pl.BlockSpec((1, tn), lambda j, i: (0, j)),
        ],
        out_specs=pl.BlockSpec((tm, tn), lambda j, i: (i, j)),
        out_shape=jax.ShapeDtypeStruct((n, f), BF16),
        scratch_shapes=[pltpu.VMEM((V7X_SUBLANES, tn), F32), pltpu.VMEM((d, tn), BF16),
                        pltpu.VMEM((d, tn), BF16)],
        compiler_params=_params(("arbitrary", "arbitrary"), pipelined, 2 * _nbytes((d, tn), BF16)),
        name="ffn_up",
    )(a, w_gate, w_up, conv_w, conv_b.reshape(1, f))


def _sc_in_kernel(a_ref, wb_ref, wc_ref, wx_ref, gb_ref, u_ref):
    a = a_ref[...]
    gb_ref[...] = jnp.dot(a, wb_ref[...], preferred_element_type=F32)
    gc = jnp.dot(a, wc_ref[...], preferred_element_type=F32)
    xi = jnp.dot(a, wx_ref[...], preferred_element_type=F32)
    u_ref[...] = gc * xi


def _sc_in(a, w_in):
    n, d = a.shape
    tm = _row_tile(n, 1024)
    tn = 512
    nj = d // tn
    pipelined = _nbytes((tm, d), BF16) + 3 * _nbytes((d, tn), BF16) + 3 * _nbytes((tm, tn), F32)
    return pl.pallas_call(
        _sc_in_kernel,
        grid=(nj, n // tm),
        in_specs=[
            pl.BlockSpec((tm, d), lambda j, i: (i, 0)),
            pl.BlockSpec((d, tn), lambda j, i: (0, j)),
            pl.BlockSpec((d, tn), lambda j, i: (0, j + nj)),
            pl.BlockSpec((d, tn), lambda j, i: (0, j + 2 * nj)),
        ],
        out_specs=[pl.BlockSpec((tm, tn), lambda j, i: (i, j))] * 2,
        out_shape=[jax.ShapeDtypeStruct((n, d), F32)] * 2,
        compiler_params=_params(("arbitrary", "arbitrary"), pipelined),
        name="sc_in",
    )(a, w_in, w_in, w_in)


def _sc_gate(u_ref, halo_ref, gb_ref, cw_ref, *, tm, seq):
    i = pl.program_id(0)
    u = u_ref[...]
    prev = jnp.where((i * tm) % seq == 0, 0.0, halo_ref[...])
    cw = cw_ref[...]
    conv = cw[0:1, :] * _shift_rows(u, prev, 2) + cw[1:2, :] * _shift_rows(u, prev, 1) + cw[2:3, :] * u
    return (gb_ref[...] * conv).astype(BF16)


def _sc_source(u, gb, conv_w, *, seq):
    d = u.shape[1]

    def specs(tm):
        assert seq % tm == 0
        hb = tm // V7X_SUBLANES
        return [
            pl.BlockSpec((tm, d), lambda i: (i, 0)),
            pl.BlockSpec((V7X_SUBLANES, d), lambda i: (jnp.maximum(i * hb - 1, 0), 0)),
            pl.BlockSpec((tm, d), lambda i: (i, 0)),
            pl.BlockSpec((CONV_WIDTH, d), lambda i: (0, 0)),
        ]

    return _Source(lambda tm: functools.partial(_sc_gate, tm=tm, seq=seq), [u, u, gb, conv_w], specs,
                   2 * d * 4, lambda tm: [], lambda tm: 2 * _nbytes((tm, d), F32), "sc_out")


def _sg_in_kernel(a_ref, wu_ref, wv_ref, u_ref, v_ref):
    a = a_ref[...]
    u_ref[...] = jax.nn.gelu(jnp.dot(a, wu_ref[...], preferred_element_type=F32))
    v_ref[...] = jax.nn.gelu(jnp.dot(a, wv_ref[...], preferred_element_type=F32))


def _sg_in(a, w_in):
    n, d = a.shape
    width = w_in.shape[1] // 2
    tm = _row_tile(n, 1024)
    tn = 512
    nj = width // tn
    pipelined = _nbytes((tm, d), BF16) + 2 * _nbytes((d, tn), BF16) + 2 * _nbytes((tm, tn), F32)
    return pl.pallas_call(
        _sg_in_kernel,
        grid=(nj, n // tm),
        in_specs=[
            pl.BlockSpec((tm, d), lambda j, i: (i, 0)),
            pl.BlockSpec((d, tn), lambda j, i: (0, j)),
            pl.BlockSpec((d, tn), lambda j, i: (0, j + nj)),
        ],
        out_specs=[pl.BlockSpec((tm, tn), lambda j, i: (i, j))] * 2,
        out_shape=[jax.ShapeDtypeStruct((n, width), F32)] * 2,
        compiler_params=_params(("arbitrary", "arbitrary"), pipelined),
        name="sg_in",
    )(a, w_in, w_in)


def _sg_gate(u_ref, v_ref, lng_ref, lnb_ref, ws_ref, bs_ref, vn_ref, y_ref, *, tm):
    v = v_ref[...]
    mu = jnp.mean(v, axis=-1, keepdims=True)
    vc = v - mu
    var = jnp.mean(vc * vc, axis=-1, keepdims=True)
    vn_ref[...] = (vc * lax.rsqrt(var + EPS) * lng_ref[...] + lnb_ref[...]).astype(vn_ref.dtype)
    t = SGU_CHUNK
    nchunk = tm // t
    row = lax.broadcasted_iota(jnp.int32, (t, t), 0)
    col = lax.broadcasted_iota(jnp.int32, (t, t), 1)
    bs = bs_ref[...]
    for g in range(SGU_GROUPS):
        cols = slice(g * t, (g + 1) * t)
        w = jnp.where(row >= col, ws_ref[g], 0.0).astype(BF16)
        rhs = jnp.concatenate([vn_ref[c * t:(c + 1) * t, cols] for c in range(nchunk)], axis=1)
        mixed = jnp.dot(w, rhs, preferred_element_type=F32) + bs[:, g:g + 1]
        for c in range(nchunk):
            rows = slice(c * t, (c + 1) * t)
            y_ref[rows, cols] = (u_ref[rows, cols] * mixed[:, c * t:(c + 1) * t]).astype(y_ref.dtype)
    return y_ref[...]


def _sg_source(u, v, ln_g, ln_b, w_s, b_s):
    width = u.shape[1]

    def specs(tm):
        assert tm % SGU_CHUNK == 0
        return [
            pl.BlockSpec((tm, width), lambda i: (i, 0)),
            pl.BlockSpec((tm, width), lambda i: (i, 0)),
            pl.BlockSpec((1, width), lambda i: (0, 0)),
            pl.BlockSpec((1, width), lambda i: (0, 0)),
            pl.BlockSpec(w_s.shape, lambda i: (0, 0, 0)),
            pl.BlockSpec((SGU_CHUNK, SGU_GROUPS), lambda i: (0, 0)),
        ]

    return _Source(lambda tm: functools.partial(_sg_gate, tm=tm),
                   [u, v, ln_g.reshape(1, width), ln_b.reshape(1, width), w_s, b_s.T], specs,
                   2 * width * 4, lambda tm: [pltpu.VMEM((tm, width), BF16)] * 2,
                   lambda tm: 2 * _nbytes(w_s.shape, F32) + 2 * _nbytes((tm, width), BF16), "sg_out")


DIFF_TQ = 512
DIFF_TK = 512
DIFF_HEADS_PER_STEP = 2
DIFF_CHUNK = 256
SB_TQ = 256
SB_TK = 256
SB_HEADS_PER_STEP = 4
SB_DEAD_LOG2 = -160.0
LOG2E = 1.4426950408889634


def _rope_table_kernel(pos_ref, invf_ref, sign_ref, cos_ref, sin_ref):
    ang = pos_ref[...] * invf_ref[...]
    cos_ref[...] = jnp.cos(ang)
    sin_ref[...] = jnp.sin(ang) * sign_ref[...]


def _rope_tables(positions, head_dim):
    n = positions.size
    half = head_dim // 2
    inv_freq = 1.0 / (ROPE_THETA ** (jnp.arange(0, head_dim, 2, dtype=F32) / head_dim))
    lane = jnp.arange(V7X_LANES)
    invf = inv_freq[lane % half].reshape(1, V7X_LANES)
    sign = jnp.where(lane % head_dim < half, -1.0, 1.0).astype(F32).reshape(1, V7X_LANES)
    tm = _row_tile(n, 1024)
    return pl.pallas_call(
        _rope_table_kernel,
        grid=(n // tm,),
        in_specs=[
            pl.BlockSpec((tm, 1), lambda i: (i, 0)),
            pl.BlockSpec((1, V7X_LANES), lambda i: (0, 0)),
            pl.BlockSpec((1, V7X_LANES), lambda i: (0, 0)),
        ],
        out_specs=[pl.BlockSpec((tm, V7X_LANES), lambda i: (i, 0))] * 2,
        out_shape=[jax.ShapeDtypeStruct((n, V7X_LANES), F32)] * 2,
        compiler_params=_params(("arbitrary",), 3 * _nbytes((tm, V7X_LANES), F32)),
        name="rope_tables",
    )(positions.astype(F32).reshape(n, 1), invf, sign)


def _rotate_half(x, cos, sin, first_half):
    partner = jnp.where(first_half, pltpu.roll(x, 96, 1), pltpu.roll(x, 32, 1))
    return x * cos + partner * sin


def _qkv_kernel(a_ref, wq_ref, wk_ref, wv_ref, *refs, rope, q_scale, tk):
    if rope:
        cos_ref, sin_ref, q_ref, k_ref, vt_ref = refs
    else:
        q_ref, k_ref, vt_ref = refs
    a = a_ref[...]
    q = jnp.dot(a, wq_ref[...], preferred_element_type=F32)
    k = jnp.dot(a, wk_ref[...], preferred_element_type=F32)
    v = jnp.dot(a, wv_ref[...], preferred_element_type=F32)
    tm, tn = q.shape
    if rope:
        cos = cos_ref[...]
        sin = sin_ref[...]
        lane = lax.broadcasted_iota(jnp.int32, cos.shape, 1)
        first_half = (lane % 64) < 32
        for c in range(tn // V7X_LANES):
            cols = slice(c * V7X_LANES, (c + 1) * V7X_LANES)
            q_ref[:, cols] = (_rotate_half(q[:, cols], cos, sin, first_half) * q_scale).astype(q_ref.dtype)
            k_ref[:, cols] = _rotate_half(k[:, cols], cos, sin, first_half).astype(k_ref.dtype)
    else:
        q_ref[...] = q.astype(q_ref.dtype)
        k_ref[...] = k.astype(k_ref.dtype)
    for hh in range(tn // HEAD_WIDTH):
        for kk in range(tm // tk):
            blk = v[kk * tk:(kk + 1) * tk, hh * HEAD_WIDTH:(hh + 1) * HEAD_WIDTH]
            vt_ref[hh, kk] = blk.T.astype(vt_ref.dtype)


def _qkv(a, w_qkv, *, tk, rope_tables=None, q_scale=1.0):
    n, d = a.shape
    width = w_qkv.shape[1] // 3
    tm = _row_tile(n, 1024)
    tn = 512
    nj = width // tn
    rope = rope_tables is not None
    in_specs = [
        pl.BlockSpec((tm, d), lambda j, i: (i, 0)),
        pl.BlockSpec((d, tn), lambda j, i: (0, j)),
        pl.BlockSpec((d, tn), lambda j, i: (0, j + nj)),
        pl.BlockSpec((d, tn), lambda j, i: (0, j + 2 * nj)),
    ]
    args = [a, w_qkv, w_qkv, w_qkv]
    if rope:
        in_specs += [pl.BlockSpec((tm, V7X_LANES), lambda j, i: (i, 0))] * 2
        args += list(rope_tables)
    heads = width // HEAD_WIDTH
    pipelined = (_nbytes((tm, d), BF16) + 3 * _nbytes((d, tn), BF16) + 3 * _nbytes((tm, tn), BF16)
                 + 3 * _nbytes((tm, tn), F32) + 2 * _nbytes((tm, V7X_LANES), F32))
    return pl.pallas_call(
        functools.partial(_qkv_kernel, rope=rope, q_scale=q_scale, tk=tk),
        grid=(nj, n // tm),
        in_specs=in_specs,
        out_specs=[
            pl.BlockSpec((tm, tn), lambda j, i: (i, j)),
            pl.BlockSpec((tm, tn), lambda j, i: (i, j)),
            pl.BlockSpec((tn // HEAD_WIDTH, tm // tk, HEAD_WIDTH, tk), lambda j, i: (j, i, 0, 0)),
        ],
        out_shape=[
            jax.ShapeDtypeStruct((n, width), BF16),
            jax.ShapeDtypeStruct((n, width), BF16),
            jax.ShapeDtypeStruct((heads, n // tk, HEAD_WIDTH, tk), BF16),
        ],
        compiler_params=_params(("arbitrary", "arbitrary"), pipelined),
        name="qkv_rope" if rope else "qkv",
    )(*args)


def _attn_specs(seq, tq, tk, heads):
    nq = seq // tq
    nk = seq // tk
    w = heads * HEAD_WIDTH
    q_spec = pl.BlockSpec((tq, w), lambda b, h, qi: (b * nq + qi, h))
    k_specs = [pl.BlockSpec((seq, HEAD_WIDTH), functools.partial(lambda b, h, qi, hh: (b, h * heads + hh), hh=hh))
               for hh in range(heads)]
    vt_spec = pl.BlockSpec((heads, nk, HEAD_WIDTH, tk), lambda b, h, qi: (h, b, 0, 0))
    o_spec = pl.BlockSpec((tq, w), lambda b, h, qi: (b * nq + qi, h))
    return q_spec, k_specs, vt_spec, o_spec


def _attn_params(seq, tq, tk, heads, score_cols):
    w = heads * HEAD_WIDTH
    pipelined = 2 * _nbytes((tq, w), BF16) + 2 * _nbytes((seq, w), BF16)
    scratch = (_nbytes((2, heads, tk, score_cols), F32) + _nbytes((2, heads, tk, score_cols), BF16)
               + _nbytes((heads, HEAD_WIDTH, score_cols), F32))
    return _params(("arbitrary", "arbitrary", "arbitrary"), pipelined, scratch)


def _head_cols(hh):
    return slice(hh * HEAD_WIDTH, (hh + 1) * HEAD_WIDTH)


def _diff_attn_kernel(q_ref, *refs, tq, tk, heads, lambda_init):
    k_refs = refs[:heads]
    (vt_ref, lq1_ref, lk1_ref, lq2_ref, lk2_ref, g_ref, o_ref,
     qq_ref, s_ref, p_ref, alpha_ref, m_ref, l_ref, acc_ref) = refs[heads:]
    qi = pl.program_id(2)
    feature = lax.broadcasted_iota(jnp.int32, (HEAD_WIDTH, tq), 0)
    for hh in range(heads):
        qt = q_ref[:, _head_cols(hh)].astype(F32).T
        qq_ref[hh, :, :tq] = jnp.where(feature < 64, qt, 0.0).astype(BF16)
        qq_ref[hh, :, tq:] = jnp.where(feature >= 64, qt, 0.0).astype(BF16)
        m_ref[hh] = jnp.full(m_ref.shape[1:], -jnp.inf, F32)
        l_ref[hh] = jnp.zeros(l_ref.shape[1:], F32)
        acc_ref[hh] = jnp.zeros(acc_ref.shape[1:], F32)

    chunks = [(hh, slice(c, c + DIFF_CHUNK)) for hh in range(heads) for c in range(0, 2 * tq, DIFF_CHUNK)]

    def scores_to(slot, kb, hh, cols):
        rows = pl.ds(pl.multiple_of(kb * tk, tk), tk)
        s_ref[slot, hh, cols.start // DIFF_CHUNK] = jnp.dot(
            k_refs[hh][rows, :], qq_ref[hh, :, cols], preferred_element_type=F32)

    def accumulate(slot, kb, hh, cols):
        ci = cols.start // DIFF_CHUNK
        acc_ref[hh, ci] = alpha_ref[slot, hh, :, cols] * acc_ref[hh, ci] + jnp.dot(
            vt_ref[hh, kb], p_ref[slot, hh, cols.start // DIFF_CHUNK], preferred_element_type=F32)

    def softmax_to(slot, kb, hh, cols, diagonal):
        s = s_ref[slot, hh, cols.start // DIFF_CHUNK]
        if diagonal:
            key = kb * tk + lax.broadcasted_iota(jnp.int32, s.shape, 0)
            qry = qi * tq + cols.start % tq + lax.broadcasted_iota(jnp.int32, s.shape, 1)
            s = jnp.where(key <= qry, s, -jnp.inf)
        m = m_ref[hh, :, cols]
        m_new = jnp.maximum(m, jnp.max(s, axis=0, keepdims=True))
        alpha = jnp.exp2(m - m_new)
        p = jnp.exp2(s - m_new)
        m_ref[hh, :, cols] = m_new
        l_ref[hh, :, cols] = alpha * l_ref[hh, :, cols] + jnp.sum(p, axis=0, keepdims=True)
        alpha_ref[slot, hh, :, cols] = alpha
        p_ref[slot, hh, cols.start // DIFF_CHUNK] = p.astype(BF16)

    def step(slot, kb, prev_kb, diagonal):
        other = 1 - slot
        for hh, cols in chunks:
            if not diagonal:
                accumulate(other, prev_kb, hh, cols)
            scores_to(other, jnp.maximum(kb - 1, 0), hh, cols)
            softmax_to(slot, kb, hh, cols, diagonal)

    nfull = (qi * tq) // tk
    for hh, cols in chunks:
        scores_to(0, nfull, hh, cols)
    step(0, nfull, None, True)

    def body(t, _):
        kb = nfull - 1 - t

        @pl.when(t % 2 == 0)
        def _():
            step(1, kb, kb + 1, False)

        @pl.when(t % 2 == 1)
        def _():
            step(0, kb, kb + 1, False)

        return 0

    lax.fori_loop(0, nfull, body, 0)

    @pl.when(nfull % 2 == 0)
    def _():
        for hh, cols in chunks:
            accumulate(0, 0, hh, cols)

    @pl.when(nfull % 2 == 1)
    def _():
        for hh, cols in chunks:
            accumulate(1, 0, hh, cols)

    lam = (jnp.exp(jnp.sum(lq1_ref[...] * lk1_ref[...], axis=-1, keepdims=True))
           - jnp.exp(jnp.sum(lq2_ref[...] * lk2_ref[...], axis=-1, keepdims=True)) + lambda_init)
    for hh in range(heads):
        for c in range(0, tq, DIFF_CHUNK):
            first, second = slice(c, c + DIFF_CHUNK), slice(tq + c, tq + c + DIFF_CHUNK)
            o = (acc_ref[hh, first.start // DIFF_CHUNK] / l_ref[hh, :, first]
                 - lam * (acc_ref[hh, second.start // DIFF_CHUNK] / l_ref[hh, :, second]))
            ms = jnp.mean(o * o, axis=0, keepdims=True)
            o = o * lax.rsqrt(ms + EPS) * g_ref[...] * (1.0 - lambda_init)
            o_ref[first, _head_cols(hh)] = o.T.astype(o_ref.dtype)


def _diff_attn(q, k, vt, lq1, lk1, lq2, lk2, subln_g, *, batch, seq, lambda_init):
    n, width = q.shape
    tq, tk, heads = DIFF_TQ, DIFF_TK, DIFF_HEADS_PER_STEP
    q_spec, k_specs, vt_spec, o_spec = _attn_specs(seq, tq, tk, heads)
    dd = lq1.shape[0]
    small = pl.BlockSpec((1, dd), lambda b, h, qi: (0, 0))
    return pl.pallas_call(
        functools.partial(_diff_attn_kernel, tq=tq, tk=tk, heads=heads, lambda_init=lambda_init),
        grid=(batch, width // (heads * HEAD_WIDTH), seq // tq),
        in_specs=[q_spec, *k_specs, vt_spec, small, small, small, small,
                  pl.BlockSpec((HEAD_WIDTH, 1), lambda b, h, qi: (0, 0))],
        out_specs=o_spec,
        out_shape=jax.ShapeDtypeStruct((n, width), BF16),
        scratch_shapes=[
            pltpu.VMEM((heads, HEAD_WIDTH, 2 * tq), BF16),
            pltpu.VMEM((2, heads, 2 * tq // DIFF_CHUNK, tk, DIFF_CHUNK), F32),
            pltpu.VMEM((2, heads, 2 * tq // DIFF_CHUNK, tk, DIFF_CHUNK), BF16),
            pltpu.VMEM((2, heads, 1, 2 * tq), F32),
            pltpu.VMEM((heads, 1, 2 * tq), F32),
            pltpu.VMEM((heads, 1, 2 * tq), F32),
            pltpu.VMEM((heads, 2 * tq // DIFF_CHUNK, HEAD_WIDTH, DIFF_CHUNK), F32),
        ],
        compiler_params=_attn_params(seq, tq, tk, heads, 2 * tq),
        name="diff_attn",
    )(q, *[k] * heads, vt, lq1.reshape(1, dd), lk1.reshape(1, dd), lq2.reshape(1, dd), lk2.reshape(1, dd),
      subln_g.reshape(HEAD_WIDTH, 1))


def _sb_attn_kernel(q_ref, *refs, tq, tk, heads, scale):
    k_refs = refs[:heads]
    vt_ref, o_ref, qt_ref, later_ref, z_ref, tail_ref, acc_ref = refs[heads:]
    qi = pl.program_id(2)
    row = lax.broadcasted_iota(jnp.int32, (tk, tk), 0)
    col = lax.broadcasted_iota(jnp.int32, (tk, tk), 1)
    later_ref[...] = jnp.where(col > row, -1.0, 0.0).astype(BF16)
    for hh in range(heads):
        qt_ref[hh] = q_ref[:, _head_cols(hh)].astype(F32).T.astype(BF16)
    tail_ref[...] = jnp.zeros_like(tail_ref)
    acc_ref[...] = jnp.zeros_like(acc_ref)

    def scores_to(slot, kb):
        rows = pl.ds(pl.multiple_of(kb * tk, tk), tk)
        for hh in range(heads):
            z_ref[slot, hh] = jnp.dot(k_refs[hh][rows, :], qt_ref[hh],
                                      preferred_element_type=F32) * (scale * LOG2E)

    def visit(blocks):
        neg_later = later_ref[...]
        his, los, colsums = {}, {}, {}
        for slot, _, mask in blocks:
            for hh in range(heads):
                z = z_ref[slot, hh]
                softplus = jnp.maximum(z, 0.0) + jnp.log2(1.0 + jnp.exp2(-jnp.abs(z)))
                z_ref[slot, hh] = z - softplus
                if mask is not None:
                    softplus = jnp.where(mask, softplus, 0.0)
                hi = softplus.astype(BF16)
                his[slot, hh] = hi
                los[slot, hh] = (softplus - hi.astype(F32)).astype(BF16)
                colsums[slot, hh] = jnp.sum(softplus, axis=0, keepdims=True)
        survives = {key: jnp.dot(neg_later, his[key], preferred_element_type=F32)
                    + jnp.dot(neg_later, los[key], preferred_element_type=F32) for key in his}
        weights = {}
        for hh in range(heads):
            tail = tail_ref[hh]
            for slot, _, mask in blocks:
                a = jnp.exp2(z_ref[slot, hh] + survives[slot, hh] + tail)
                if mask is not None:
                    a = jnp.where(mask, a, 0.0)
                weights[slot, hh] = a.astype(BF16)
                tail = tail - colsums[slot, hh]
            tail_ref[hh] = tail
        for slot, kb, _ in blocks:
            for hh in range(heads):
                acc_ref[hh] += jnp.dot(vt_ref[hh, kb], weights[slot, hh], preferred_element_type=F32)

    key = qi * tk + lax.broadcasted_iota(jnp.int32, (tk, tq), 0)
    qry = qi * tq + lax.broadcasted_iota(jnp.int32, (tk, tq), 1)
    before = jnp.maximum(qi - 1, 0)
    scores_to(0, qi)
    scores_to(1, before)
    visit([(0, qi, key < qry), (1, before, jnp.broadcast_to(qi > 0, (tk, tq)))])

    def any_live():
        return jnp.max(tail_ref[...]) > SB_DEAD_LOG2

    def body(c):
        kb = qi - 2 - c[0]
        scores_to(0, kb)
        visit([(0, kb, None)])
        return c[0] + 1, any_live()

    lax.while_loop(lambda c: jnp.logical_and(c[0] < qi - 1, c[1]), body, (jnp.int32(0), any_live()))
    for hh in range(heads):
        o_ref[:, _head_cols(hh)] = acc_ref[hh].T.astype(o_ref.dtype)


def _sb_attn(q, k, vt, *, batch, seq, scale):
    n, width = q.shape
    tq, tk, heads = SB_TQ, SB_TK, SB_HEADS_PER_STEP
    assert tq == tk
    q_spec, k_specs, vt_spec, o_spec = _attn_specs(seq, tq, tk, heads)
    return pl.pallas_call(
        functools.partial(_sb_attn_kernel, tq=tq, tk=tk, heads=heads, scale=scale),
        grid=(batch, width // (heads * HEAD_WIDTH), seq // tq),
        in_specs=[q_spec, *k_specs, vt_spec],
        out_specs=o_spec,
        out_shape=jax.ShapeDtypeStruct((n, width), BF16),
        scratch_shapes=[
            pltpu.VMEM((heads, HEAD_WIDTH, tq), BF16),
            pltpu.VMEM((tk, tk), BF16),
            pltpu.VMEM((2, heads, tk, tq), F32),
            pltpu.VMEM((heads, 1, tq), F32),
            pltpu.VMEM((heads, HEAD_WIDTH, tq), F32),
        ],
        compiler_params=_attn_params(seq, tq, tk, heads, tq),
        name="sb_attn",
    )(q, *[k] * heads, vt)


def kernel(x, positions, norm_mix_g, norm_ffn_g, norm_final_g, sc_w_in, sc_conv_w, sc_w_out, sg_w_in, sg_ln_g, sg_ln_b, sg_w_s, sg_b_s, sg_w_out, da_w_qkv, da_lambda_q1, da_lambda_k1, da_lambda_q2, da_lambda_k2, da_subln_g, da_w_out, sb_w_qkv, sb_w_out, ffn_w_gate, ffn_w_up, ffn_conv_w, ffn_conv_b, ffn_w_down):
    batch, seq, d = x.shape
    depth = norm_mix_g.shape[0]
    n = batch * seq
    assert seq % DIFF_TK == 0 and seq % DIFF_TQ == 0 and seq % SB_TQ == 0 and d % HEAD_WIDTH == 0

    def bf(w):
        return w.astype(BF16)

    w_down = bf(ffn_w_down)
    h = x.reshape(n, d)
    a = _rmsnorm(h, norm_mix_g[0])
    for layer in range(depth):
        mixer, j = layer % 4, layer // 4
        if mixer == 0:
            gb, u = _sc_in(a, bf(sc_w_in[j]))
            y = _sc_source(u, gb, sc_conv_w[j], seq=seq)
            w_out = sc_w_out
        elif mixer == 1:
            u, v = _sg_in(a, bf(sg_w_in[j]))
            y = _sg_source(u, v, sg_ln_g[j], sg_ln_b[j], sg_w_s[j], sg_b_s[j])
            w_out = sg_w_out
        elif mixer == 2:
            head_dim = d // DIFF_HEADS // 2
            lambda_init = 0.8 - 0.6 * math.exp(-0.3 * layer)
            tables = _rope_tables(positions, head_dim)
            q, k, vt = _qkv(a, bf(da_w_qkv[j]), tk=DIFF_TK, rope_tables=tables,
                            q_scale=head_dim ** -0.5 * LOG2E)
            y = _diff_attn(q, k, vt, da_lambda_q1[j], da_lambda_k1[j], da_lambda_q2[j], da_lambda_k2[j],
                           da_subln_g[j], batch=batch, seq=seq, lambda_init=lambda_init)
            w_out = da_w_out
        else:
            q, k, vt = _qkv(a, bf(sb_w_qkv[j]), tk=SB_TK)
            y = _sb_attn(q, k, vt, batch=batch, seq=seq, scale=(d // SB_HEADS) ** -0.5)
            w_out = sb_w_out
        h, a = _out_proj(y, bf(w_out), j, h, norm_ffn_g[layer])
        t = _ffn_up(a, ffn_w_gate, ffn_w_up, layer, ffn_conv_w[layer], ffn_conv_b[layer], seq=seq)
        last = layer == depth - 1
        h, a = _out_proj(t, w_down, layer, h, norm_final_g if last else norm_mix_g[layer + 1], final=last)
    return a.reshape(batch, seq, d)
```

```python
import functools
import math
from typing import Callable, NamedTuple, Sequence

import jax
import jax.numpy as jnp
from jax import lax
from jax.experimental import pallas as pl
from jax.experimental.pallas import tpu as pltpu

F32 = jnp.float32
BF16 = jnp.bfloat16

EPS = 1e-6
ROPE_THETA = 10000.0
CONV_WIDTH = 3
SGU_GROUPS = 16
SGU_CHUNK = 128
DIFF_HEADS = 16
SB_HEADS = 16
HEAD_WIDTH = 128

V7X_LANES = 128
V7X_SUBLANES = 8
V7X_VMEM_BYTES = 64 * 1024 * 1024
INTERNAL_SCRATCH_BYTES = 12 * 1024 * 1024


def _nbytes(shape, dtype):
    return math.prod(shape) * jnp.dtype(dtype).itemsize


def _params(semantics, pipelined_bytes, resident_bytes=0):
    limit = 2 * pipelined_bytes + resident_bytes + INTERNAL_SCRATCH_BYTES
    limit = min(limit, V7X_VMEM_BYTES - 4 * 1024 * 1024)
    return pltpu.CompilerParams(dimension_semantics=semantics, vmem_limit_bytes=int(limit))


def _row_tile(n, want):
    t = min(n, want)
    assert n % t == 0, (n, t)
    return t


def _rms(x, g):
    ms = jnp.mean(x * x, axis=-1, keepdims=True)
    return x * lax.rsqrt(ms + EPS) * g


def _shift_rows(x, prev, k):
    row = lax.broadcasted_iota(jnp.int32, x.shape, 0)
    out = pltpu.roll(x, k, 0)
    for r in range(k):
        out = jnp.where(row == r, prev[V7X_SUBLANES - k + r:V7X_SUBLANES - k + r + 1, :], out)
    return out


def _rmsnorm_kernel(x_ref, g_ref, o_ref):
    o_ref[...] = _rms(x_ref[...], g_ref[...]).astype(o_ref.dtype)


def _rmsnorm(h, g):
    n, d = h.shape
    tm = _row_tile(n, 512)
    return pl.pallas_call(
        _rmsnorm_kernel,
        grid=(n // tm,),
        in_specs=[pl.BlockSpec((tm, d), lambda i: (i, 0)), pl.BlockSpec((1, d), lambda i: (0, 0))],
        out_specs=pl.BlockSpec((tm, d), lambda i: (i, 0)),
        out_shape=jax.ShapeDtypeStruct((n, d), BF16),
        compiler_params=_params(("arbitrary",), _nbytes((tm, d), F32) + _nbytes((tm, d), BF16)),
        name="rmsnorm",
    )(h, g.reshape(1, d))


class _Source(NamedTuple):
    make_x: Callable
    inputs: Sequence
    specs: Callable
    row_bytes: int
    scratch: Callable
    resident_bytes: Callable
    name: str


def _plain_source(x):
    k = x.shape[1]
    return _Source(lambda tm: lambda x_ref: x_ref[...], [x],
                   lambda tm: [pl.BlockSpec((tm, k), lambda i: (i, 0))],
                   k * 2, lambda tm: [], lambda tm: 0, "out_proj")


def _out_proj_kernel(*refs, n_src, n_scratch, make_x, emit_h):
    src = refs[:n_src]
    w_ref, h_ref, g_ref = refs[n_src:n_src + 3]
    out_refs = refs[n_src + 3:len(refs) - n_scratch]
    scratch = refs[len(refs) - n_scratch:]
    hn = h_ref[...] + jnp.dot(make_x(*src, *scratch), w_ref[...], preferred_element_type=F32)
    if emit_h:
        out_refs[0][...] = hn
    an_ref = out_refs[-1]
    an_ref[...] = _rms(hn, g_ref[...]).astype(an_ref.dtype)


def _out_proj(source, w, layer, h, g, *, final=False):
    if not isinstance(source, _Source):
        source = _plain_source(source)
    n, d = h.shape
    k = w.shape[1]
    an_dtype = F32 if final else BF16

    def pipelined_bytes(tm):
        return tm * source.row_bytes + 2 * _nbytes((tm, d), F32) + _nbytes((tm, d), an_dtype)

    def resident_bytes(tm):
        return _nbytes((k, d), BF16) + source.resident_bytes(tm)

    budget = V7X_VMEM_BYTES - 4 * 1024 * 1024 - INTERNAL_SCRATCH_BYTES
    tm = next(t for t in (512, 256, 128)
              if n % t == 0 and 2 * pipelined_bytes(t) + resident_bytes(t) <= budget)
    out_shape = [jax.ShapeDtypeStruct((n, d), an_dtype)]
    out_specs = [pl.BlockSpec((tm, d), lambda i: (i, 0))]
    if not final:
        out_shape.insert(0, jax.ShapeDtypeStruct((n, d), F32))
        out_specs.insert(0, pl.BlockSpec((tm, d), lambda i: (i, 0)))
    scratch = source.scratch(tm)
    outs = pl.pallas_call(
        functools.partial(_out_proj_kernel, n_src=len(source.inputs), n_scratch=len(scratch),
                          make_x=source.make_x(tm), emit_h=not final),
        grid=(n // tm,),
        in_specs=[
            *source.specs(tm),
            pl.BlockSpec((None, k, d), lambda i: (layer, 0, 0), pipeline_mode=pl.Buffered(1)),
            pl.BlockSpec((tm, d), lambda i: (i, 0)),
            pl.BlockSpec((1, d), lambda i: (0, 0)),
        ],
        out_specs=out_specs,
        out_shape=out_shape,
        scratch_shapes=scratch,
        compiler_params=_params(("arbitrary",), pipelined_bytes(tm), resident_bytes(tm)),
        name=source.name,
    )(*source.inputs, w, h, g.reshape(1, d))
    return (None, outs[0]) if final else (outs[0], outs[1])


def _ffn_up_kernel(a_ref, wg_ref, wu_ref, cw_ref, cb_ref, t_ref, carry_ref, wg_bf_ref, wu_bf_ref, *,
                   tm, seq):
    i = pl.program_id(1)

    @pl.when(i == 0)
    def _():
        wg_bf_ref[...] = wg_ref[...].astype(BF16)
        wu_bf_ref[...] = wu_ref[...].astype(BF16)

    @pl.when((i * tm) % seq == 0)
    def _():
        carry_ref[...] = jnp.zeros_like(carry_ref)

    a = a_ref[...]
    g = jnp.dot(a, wg_bf_ref[...], preferred_element_type=F32)
    up = jnp.dot(a, wu_bf_ref[...], preferred_element_type=F32)
    prev = carry_ref[...]
    cw = cw_ref[...]
    conv = (cw[0:1, :] * _shift_rows(g, prev, 2) + cw[1:2, :] * _shift_rows(g, prev, 1)
            + cw[2:3, :] * g + cb_ref[...])
    t_ref[...] = (conv * jax.nn.sigmoid(conv) * up).astype(t_ref.dtype)
    carry_ref[...] = g[tm - V7X_SUBLANES:, :]


def _ffn_up(a, w_gate, w_up, layer, conv_w, conv_b, *, seq):
    n, d = a.shape
    f = w_gate.shape[2]
    tm = _row_tile(seq, 1024)
    tn = 512
    assert f % tn == 0
    pipelined = (_nbytes((tm, d), BF16) + 2 * _nbytes((d, tn), F32) + _nbytes((tm, tn), BF16)
                 + 2 * _nbytes((tm, tn), F32))
    w_spec = pl.BlockSpec((None, d, tn), lambda j, i: (layer, 0, j))
    return pl.pallas_call(
        functools.partial(_ffn_up_kernel, tm=tm, seq=seq),
        grid=(f // tn, n // tm),
        in_specs=[
            pl.BlockSpec((tm, d), lambda j, i: (i, 0)),
            w_spec,
            w_spec,
            pl.BlockSpec((CONV_WIDTH, tn), lambda j, i: (0, j)),
            pl.BlockSpec((1, tn), lambda j, i: (0, j)),
        ],
        out_specs=pl.BlockSpec((tm, tn), lambda j, i: (i, j)),
        out_shape=jax.ShapeDtypeStruct((n, f), BF16),
        scratch_shapes=[pltpu.VMEM((V7X_SUBLANES, tn), F32), pltpu.VMEM((d, tn), BF16),
                        pltpu.VMEM((d, tn), BF16)],
        compiler_params=_params(("arbitrary", "arbitrary"), pipelined, 2 * _nbytes((d, tn), BF16)),
        name="ffn_up",
    )(a, w_gate, w_up, conv_w, conv_b.reshape(1, f))


def _sc_in_kernel(a_ref, wb_ref, wc_ref, wx_ref, gb_ref, u_ref):
    a = a_ref[...]
    gb_ref[...] = jnp.dot(a, wb_ref[...], preferred_element_type=F32)
    gc = jnp.dot(a, wc_ref[...], preferred_element_type=F32)
    xi = jnp.dot(a, wx_ref[...], preferred_element_type=F32)
    u_ref[...] = gc * xi


def _sc_in(a, w_in):
    n, d = a.shape
    tm = _row_tile(n, 1024)
    tn = 512
    nj = d // tn
    pipelined = _nbytes((tm, d), BF16) + 3 * _nbytes((d, tn), BF16) + 3 * _nbytes((tm, tn), F32)
    return pl.pallas_call(
        _sc_in_kernel,
        grid=(nj, n // tm),
        in_specs=[
            pl.BlockSpec((tm, d), lambda j, i: (i, 0)),
            pl.BlockSpec((d, tn), lambda j, i: (0, j)),
            pl.BlockSpec((d, tn), lambda j, i: (0, j + nj)),
            pl.BlockSpec((d, tn), lambda j, i: (0, j + 2 * nj)),
        ],
        out_specs=[pl.BlockSpec((tm, tn), lambda j, i: (i, j))] * 2,
        out_shape=[jax.ShapeDtypeStruct((n, d), F32)] * 2,
        compiler_params=_params(("arbitrary", "arbitrary"), pipelined),
        name="sc_in",
    )(a, w_in, w_in, w_in)


def _sc_gate(u_ref, halo_ref, gb_ref, cw_ref, *, tm, seq):
    i = pl.program_id(0)
    u = u_ref[...]
    prev = jnp.where((i * tm) % seq == 0, 0.0, halo_ref[...])
    cw = cw_ref[...]
    conv = cw[0:1, :] * _shift_rows(u, prev, 2) + cw[1:2, :] * _shift_rows(u, prev, 1) + cw[2:3, :] * u
    return (gb_ref[...] * conv).astype(BF16)


def _sc_source(u, gb, conv_w, *, seq):
    d = u.shape[1]

    def specs(tm):
        assert seq % tm == 0
        hb = tm // V7X_SUBLANES
        return [
            pl.BlockSpec((tm, d), lambda i: (i, 0)),
            pl.BlockSpec((V7X_SUBLANES, d), lambda i: (jnp.maximum(i * hb - 1, 0), 0)),
            pl.BlockSpec((tm, d), lambda i: (i, 0)),
            pl.BlockSpec((CONV_WIDTH, d), lambda i: (0, 0)),
        ]

    return _Source(lambda tm: functools.partial(_sc_gate, tm=tm, seq=seq), [u, u, gb, conv_w], specs,
                   2 * d * 4, lambda tm: [], lambda tm: 2 * _nbytes((tm, d), F32), "sc_out")


def _sg_in_kernel(a_ref, wu_ref, wv_ref, u_ref, v_ref):
    a = a_ref[...]
    u_ref[...] = jax.nn.gelu(jnp.dot(a, wu_ref[...], preferred_element_type=F32))
    v_ref[...] = jax.nn.gelu(jnp.dot(a, wv_ref[...], preferred_element_type=F32))


def _sg_in(a, w_in):
    n, d = a.shape
    width = w_in.shape[1] // 2
    tm = _row_tile(n, 1024)
    tn = 512
    nj = width // tn
    pipelined = _nbytes((tm, d), BF16) + 2 * _nbytes((d, tn), BF16) + 2 * _nbytes((tm, tn), F32)
    return pl.pallas_call(
        _sg_in_kernel,
        grid=(nj, n // tm),
        in_specs=[
            pl.BlockSpec((tm, d), lambda j, i: (i, 0)),
            pl.BlockSpec((d, tn), lambda j, i: (0, j)),
            pl.BlockSpec((d, tn), lambda j, i: (0, j + nj)),
        ],
        out_specs=[pl.BlockSpec((tm, tn), lambda j, i: (i, j))] * 2,
        out_shape=[jax.ShapeDtypeStruct((n, width), F32)] * 2,
        compiler_params=_params(("arbitrary", "arbitrary"), pipelined),
        name="sg_in",
    )(a, w_in, w_in)


def _sg_gate(u_ref, v_ref, lng_ref, lnb_ref, ws_ref, bs_ref, vn_ref, y_ref, *, tm):
    v = v_ref[...]
    mu = jnp.mean(v, axis=-1, keepdims=True)
    vc = v - mu
    var = jnp.mean(vc * vc, axis=-1, keepdims=True)
    vn_ref[...] = (vc * lax.rsqrt(var + EPS) * lng_ref[...] + lnb_ref[...]).astype(vn_ref.dtype)
    t = SGU_CHUNK
    nchunk = tm // t
    row = lax.broadcasted_iota(jnp.int32, (t, t), 0)
    col = lax.broadcasted_iota(jnp.int32, (t, t), 1)
    bs = bs_ref[...]
    for g in range(SGU_GROUPS):
        cols = slice(g * t, (g + 1) * t)
        w = jnp.where(row >= col, ws_ref[g], 0.0).astype(BF16)
        rhs = jnp.concatenate([vn_ref[c * t:(c + 1) * t, cols] for c in range(nchunk)], axis=1)
        mixed = jnp.dot(w, rhs, preferred_element_type=F32) + bs[:, g:g + 1]
        for c in range(nchunk):
            rows = slice(c * t, (c + 1) * t)
            y_ref[rows, cols] = (u_ref[rows, cols] * mixed[:, c * t:(c + 1) * t]).astype(y_ref.dtype)
    return y_ref[...]


def _sg_source(u, v, ln_g, ln_b, w_s, b_s):
    width = u.shape[1]

    def specs(tm):
        assert tm % SGU_CHUNK == 0
        return [
            pl.BlockSpec((tm, width), lambda i: (i, 0)),
            pl.BlockSpec((tm, width), lambda i: (i, 0)),
            pl.BlockSpec((1, width), lambda i: (0, 0)),
            pl.BlockSpec((1, width), lambda i: (0, 0)),
            pl.BlockSpec(w_s.shape, lambda i: (0, 0, 0)),
            pl.BlockSpec((SGU_CHUNK, SGU_GROUPS), lambda i: (0, 0)),
        ]

    return _Source(lambda tm: functools.partial(_sg_gate, tm=tm),
                   [u, v, ln_g.reshape(1, width), ln_b.reshape(1, width), w_s, b_s.T], specs,
                   2 * width * 4, lambda tm: [pltpu.VMEM((tm, width), BF16)] * 2,
                   lambda tm: 2 * _nbytes(w_s.shape, F32) + 2 * _nbytes((tm, width), BF16), "sg_out")


DIFF_TQ = 512
DIFF_TK = 512
DIFF_HEADS_PER_STEP = 2
DIFF_CHUNK = 256
SB_TQ = 256
SB_TK = 256
SB_HEADS_PER_STEP = 4
SB_DEAD_LOG2 = -160.0
LOG2E = 1.4426950408889634


def _rope_table_kernel(pos_ref, invf_ref, sign_ref, cos_ref, sin_ref):
    ang = pos_ref[...] * invf_ref[...]
    cos_ref[...] = jnp.cos(ang)
    sin_ref[...] = jnp.sin(ang) * sign_ref[...]


def _rope_tables(positions, head_dim):
    n = positions.size
    half = head_dim // 2
    inv_freq = 1.0 / (ROPE_THETA ** (jnp.arange(0, head_dim, 2, dtype=F32) / head_dim))
    lane = jnp.arange(V7X_LANES)
    invf = inv_freq[lane % half].reshape(1, V7X_LANES)
    sign = jnp.where(lane % head_dim < half, -1.0, 1.0).astype(F32).reshape(1, V7X_LANES)
    tm = _row_tile(n, 1024)
    return pl.pallas_call(
        _rope_table_kernel,
        grid=(n // tm,),
        in_specs=[
            pl.BlockSpec((tm, 1), lambda i: (i, 0)),
            pl.BlockSpec((1, V7X_LANES), lambda i: (0, 0)),
            pl.BlockSpec((1, V7X_LANES), lambda i: (0, 0)),
        ],
        out_specs=[pl.BlockSpec((tm, V7X_LANES), lambda i: (i, 0))] * 2,
        out_shape=[jax.ShapeDtypeStruct((n, V7X_LANES), F32)] * 2,
        compiler_params=_params(("arbitrary",), 3 * _nbytes((tm, V7X_LANES), F32)),
        name="rope_tables",
    )(positions.astype(F32).reshape(n, 1), invf, sign)


def _rotate_half(x, cos, sin, first_half):
    partner = jnp.where(first_half, pltpu.roll(x, 96, 1), pltpu.roll(x, 32, 1))
    return x * cos + partner * sin


def _qkv_kernel(a_ref, wq_ref, wk_ref, wv_ref, *refs, rope, q_scale, tk):
    if rope:
        cos_ref, sin_ref, q_ref, k_ref, vt_ref = refs
    else:
        q_ref, k_ref, vt_ref = refs
    a = a_ref[...]
    q = jnp.dot(a, wq_ref[...], preferred_element_type=F32)
    k = jnp.dot(a, wk_ref[...], preferred_element_type=F32)
    v = jnp.dot(a, wv_ref[...], preferred_element_type=F32)
    tm, tn = q.shape
    if rope:
        cos = cos_ref[...]
        sin = sin_ref[...]
        lane = lax.broadcasted_iota(jnp.int32, cos.shape, 1)
        first_half = (lane % 64) < 32
        for c in range(tn // V7X_LANES):
            cols = slice(c * V7X_LANES, (c + 1) * V7X_LANES)
            q_ref[:, cols] = (_rotate_half(q[:, cols], cos, sin, first_half) * q_scale).astype(q_ref.dtype)
            k_ref[:, cols] = _rotate_half(k[:, cols], cos, sin, first_half).astype(k_ref.dtype)
    else:
        q_ref[...] = q.astype(q_ref.dtype)
        k_ref[...] = k.astype(k_ref.dtype)
    for hh in range(tn // HEAD_WIDTH):
        for kk in range(tm // tk):
            blk = v[kk * tk:(kk + 1) * tk, hh * HEAD_WIDTH:(hh + 1) * HEAD_WIDTH]
            vt_ref[hh, kk] = blk.T.astype(vt_ref.dtype)


def _qkv(a, w_qkv, *, tk, rope_tables=None, q_scale=1.0):
    n, d = a.shape
    width = w_qkv.shape[1] // 3
    tm = _row_tile(n, 1024)
    tn = 512
    nj = width // tn
    rope = rope_tables is not None
    in_specs = [
        pl.BlockSpec((tm, d), lambda j, i: (i, 0)),
        pl.BlockSpec((d, tn), lambda j, i: (0, j)),
        pl.BlockSpec((d, tn), lambda j, i: (0, j + nj)),
        pl.BlockSpec((d, tn), lambda j, i: (0, j + 2 * nj)),
    ]
    args = [a, w_qkv, w_qkv, w_qkv]
    if rope:
        in_specs += [pl.BlockSpec((tm, V7X_LANES), lambda j, i: (i, 0))] * 2
        args += list(rope_tables)
    heads = width // HEAD_WIDTH
    pipelined = (_nbytes((tm, d), BF16) + 3 * _nbytes((d, tn), BF16) + 3 * _nbytes((tm, tn), BF16)
                 + 3 * _nbytes((tm, tn), F32) + 2 * _nbytes((tm, V7X_LANES), F32))
    return pl.pallas_call(
        functools.partial(_qkv_kernel, rope=rope, q_scale=q_scale, tk=tk),
        grid=(nj, n // tm),
        in_specs=in_specs,
        out_specs=[
            pl.BlockSpec((tm, tn), lambda j, i: (i, j)),
            pl.BlockSpec((tm, tn), lambda j, i: (i, j)),
            pl.BlockSpec((tn // HEAD_WIDTH, tm // tk, HEAD_WIDTH, tk), lambda j, i: (j, i, 0, 0)),
        ],
        out_shape=[
            jax.ShapeDtypeStruct((n, width), BF16),
            jax.ShapeDtypeStruct((n, width), BF16),
            jax.ShapeDtypeStruct((heads, n // tk, HEAD_WIDTH, tk), BF16),
        ],
        compiler_params=_params(("arbitrary", "arbitrary"), pipelined),
        name="qkv_rope" if rope else "qkv",
    )(*args)


def _attn_specs(seq, tq, tk, heads):
    nq = seq // tq
    nk = seq // tk
    w = heads * HEAD_WIDTH
    q_spec = pl.BlockSpec((tq, w), lambda b, h, qi: (b * nq + qi, h))
    k_specs = [pl.BlockSpec((seq, HEAD_WIDTH), functools.partial(lambda b, h, qi, hh: (b, h * heads + hh), hh=hh))
               for hh in range(heads)]
    vt_spec = pl.BlockSpec((heads, nk, HEAD_WIDTH, tk), lambda b, h, qi: (h, b, 0, 0))
    o_spec = pl.BlockSpec((tq, w), lambda b, h, qi: (b * nq + qi, h))
    return q_spec, k_specs, vt_spec, o_spec


def _attn_params(seq, tq, tk, heads, score_cols):
    w = heads * HEAD_WIDTH
    pipelined = 2 * _nbytes((tq, w), BF16) + 2 * _nbytes((seq, w), BF16)
    scratch = (_nbytes((2, heads, tk, score_cols), F32) + _nbytes((2, heads, tk, score_cols), BF16)
               + _nbytes((heads, HEAD_WIDTH, score_cols), F32))
    return _params(("arbitrary", "arbitrary", "arbitrary"), pipelined, scratch)


def _head_cols(hh):
    return slice(hh * HEAD_WIDTH, (hh + 1) * HEAD_WIDTH)


def _diff_attn_kernel(q_ref, *refs, tq, tk, heads, lambda_init):
    k_refs = refs[:heads]
    (vt_ref, lq1_ref, lk1_ref, lq2_ref, lk2_ref, g_ref, o_ref,
     qq_ref, s_ref, p_ref, alpha_ref, m_ref, l_ref, acc_ref) = refs[heads:]
    qi = pl.program_id(2)
    feature = lax.broadcasted_iota(jnp.int32, (HEAD_WIDTH, tq), 0)
    for hh in range(heads):
        qt = q_ref[:, _head_cols(hh)].astype(F32).T
        qq_ref[hh, :, :tq] = jnp.where(feature < 64, qt, 0.0).astype(BF16)
        qq_ref[hh, :, tq:] = jnp.where(feature >= 64, qt, 0.0).astype(BF16)
        m_ref[hh] = jnp.full(m_ref.shape[1:], -jnp.inf, F32)
        l_ref[hh] = jnp.zeros(l_ref.shape[1:], F32)
        acc_ref[hh] = jnp.zeros(acc_ref.shape[1:], F32)

    chunks = [(hh, slice(c, c + DIFF_CHUNK)) for hh in range(heads) for c in range(0, 2 * tq, DIFF_CHUNK)]

    def live_keys(cols):
        return cols.start % tq + DIFF_CHUNK

    def scores_to(slot, kb, hh, cols, nkeys=tk):
        rows = pl.ds(pl.multiple_of(kb * tk, tk), nkeys)
        s_ref[slot, hh, cols.start // DIFF_CHUNK, :nkeys, :] = jnp.dot(
            k_refs[hh][rows, :], qq_ref[hh, :, cols], preferred_element_type=F32)

    def accumulate(slot, kb, hh, cols):
        ci = cols.start // DIFF_CHUNK
        acc_ref[hh, ci] = alpha_ref[slot, hh, :, cols] * acc_ref[hh, ci] + jnp.dot(
            vt_ref[hh, kb], p_ref[slot, hh, cols.start // DIFF_CHUNK], preferred_element_type=F32)

    def softmax_to(slot, kb, hh, cols, diagonal):
        ci = cols.start // DIFF_CHUNK
        nkeys = live_keys(cols) if diagonal else tk
        s = s_ref[slot, hh, ci, :nkeys, :]
        if diagonal:
            key = kb * tk + lax.broadcasted_iota(jnp.int32, s.shape, 0)
            qry = qi * tq + cols.start % tq + lax.broadcasted_iota(jnp.int32, s.shape, 1)
            s = jnp.where(key <= qry, s, -jnp.inf)
        m = m_ref[hh, :, cols]
        m_new = jnp.maximum(m, jnp.max(s, axis=0, keepdims=True))
        alpha = jnp.exp2(m - m_new)
        p = jnp.exp2(s - m_new)
        m_ref[hh, :, cols] = m_new
        l_ref[hh, :, cols] = alpha * l_ref[hh, :, cols] + jnp.sum(p, axis=0, keepdims=True)
        alpha_ref[slot, hh, :, cols] = alpha
        p_ref[slot, hh, ci, :nkeys, :] = p.astype(BF16)
        if nkeys < tk:
            p_ref[slot, hh, ci, nkeys:, :] = jnp.zeros((tk - nkeys, DIFF_CHUNK), BF16)

    def step(slot, kb, prev_kb, diagonal):
        other = 1 - slot
        for hh, cols in chunks:
            if not diagonal:
                accumulate(other, prev_kb, hh, cols)
            scores_to(other, jnp.maximum(kb - 1, 0), hh, cols)
            softmax_to(slot, kb, hh, cols, diagonal)

    nfull = (qi * tq) // tk
    for hh, cols in chunks:
        scores_to(0, nfull, hh, cols, live_keys(cols))
    step(0, nfull, None, True)

    def body(t, _):
        kb = nfull - 1 - t

        @pl.when(t % 2 == 0)
        def _():
            step(1, kb, kb + 1, False)

        @pl.when(t % 2 == 1)
        def _():
            step(0, kb, kb + 1, False)

        return 0

    lax.fori_loop(0, nfull, body, 0)

    @pl.when(nfull % 2 == 0)
    def _():
        for hh, cols in chunks:
            accumulate(0, 0, hh, cols)

    @pl.when(nfull % 2 == 1)
    def _():
        for hh, cols in chunks:
            accumulate(1, 0, hh, cols)

    lam = (jnp.exp(jnp.sum(lq1_ref[...] * lk1_ref[...], axis=-1, keepdims=True))
           - jnp.exp(jnp.sum(lq2_ref[...] * lk2_ref[...], axis=-1, keepdims=True)) + lambda_init)
    for hh in range(heads):
        for c in range(0, tq, DIFF_CHUNK):
            first, second = slice(c, c + DIFF_CHUNK), slice(tq + c, tq + c + DIFF_CHUNK)
            o = (acc_ref[hh, first.start // DIFF_CHUNK] / l_ref[hh, :, first]
                 - lam * (acc_ref[hh, second.start // DIFF_CHUNK] / l_ref[hh, :, second]))
            ms = jnp.mean(o * o, axis=0, keepdims=True)
            o = o * lax.rsqrt(ms + EPS) * g_ref[...] * (1.0 - lambda_init)
            o_ref[first, _head_cols(hh)] = o.T.astype(o_ref.dtype)


def _diff_attn(q, k, vt, lq1, lk1, lq2, lk2, subln_g, *, batch, seq, lambda_init):
    n, width = q.shape
    tq, tk, heads = DIFF_TQ, DIFF_TK, DIFF_HEADS_PER_STEP
    assert tq == tk
    q_spec, k_specs, vt_spec, o_spec = _attn_specs(seq, tq, tk, heads)
    dd = lq1.shape[0]
    small = pl.BlockSpec((1, dd), lambda b, h, qi: (0, 0))
    return pl.pallas_call(
        functools.partial(_diff_attn_kernel, tq=tq, tk=tk, heads=heads, lambda_init=lambda_init),
        grid=(batch, width // (heads * HEAD_WIDTH), seq // tq),
        in_specs=[q_spec, *k_specs, vt_spec, small, small, small, small,
                  pl.BlockSpec((HEAD_WIDTH, 1), lambda b, h, qi: (0, 0))],
        out_specs=o_spec,
        out_shape=jax.ShapeDtypeStruct((n, width), BF16),
        scratch_shapes=[
            pltpu.VMEM((heads, HEAD_WIDTH, 2 * tq), BF16),
            pltpu.VMEM((2, heads, 2 * tq // DIFF_CHUNK, tk, DIFF_CHUNK), F32),
            pltpu.VMEM((2, heads, 2 * tq // DIFF_CHUNK, tk, DIFF_CHUNK), BF16),
            pltpu.VMEM((2, heads, 1, 2 * tq), F32),
            pltpu.VMEM((heads, 1, 2 * tq), F32),
            pltpu.VMEM((heads, 1, 2 * tq), F32),
            pltpu.VMEM((heads, 2 * tq // DIFF_CHUNK, HEAD_WIDTH, DIFF_CHUNK), F32),
        ],
        compiler_params=_attn_params(seq, tq, tk, heads, 2 * tq),
        name="diff_attn",
    )(q, *[k] * heads, vt, lq1.reshape(1, dd), lk1.reshape(1, dd), lq2.reshape(1, dd), lk2.reshape(1, dd),
      subln_g.reshape(HEAD_WIDTH, 1))


def _sb_attn_kernel(q_ref, *refs, tq, tk, heads, scale):
    k_refs = refs[:heads]
    vt_ref, o_ref, qt_ref, later_ref, z_ref, tail_ref, acc_ref = refs[heads:]
    qi = pl.program_id(2)
    row = lax.broadcasted_iota(jnp.int32, (tk, tk), 0)
    col = lax.broadcasted_iota(jnp.int32, (tk, tk), 1)
    later_ref[...] = jnp.where(col > row, -1.0, 0.0).astype(BF16)
    for hh in range(heads):
        qt_ref[hh] = q_ref[:, _head_cols(hh)].astype(F32).T.astype(BF16)
    tail_ref[...] = jnp.zeros_like(tail_ref)
    acc_ref[...] = jnp.zeros_like(acc_ref)

    def scores_to(slot, kb):
        rows = pl.ds(pl.multiple_of(kb * tk, tk), tk)
        for hh in range(heads):
            z_ref[slot, hh] = jnp.dot(k_refs[hh][rows, :], qt_ref[hh],
                                      preferred_element_type=F32) * (scale * LOG2E)

    def visit(blocks):
        neg_later = later_ref[...]
        his, los, colsums = {}, {}, {}
        for slot, _, mask in blocks:
            for hh in range(heads):
                z = z_ref[slot, hh]
                softplus = jnp.maximum(z, 0.0) + jnp.log2(1.0 + jnp.exp2(-jnp.abs(z)))
                z_ref[slot, hh] = z - softplus
                if mask is not None:
                    softplus = jnp.where(mask, softplus, 0.0)
                hi = softplus.astype(BF16)
                his[slot, hh] = hi
                los[slot, hh] = (softplus - hi.astype(F32)).astype(BF16)
                colsums[slot, hh] = jnp.sum(softplus, axis=0, keepdims=True)
        survives = {key: jnp.dot(neg_later, his[key], preferred_element_type=F32)
                    + jnp.dot(neg_later, los[key], preferred_element_type=F32) for key in his}
        weights = {}
        for hh in range(heads):
            tail = tail_ref[hh]
            for slot, _, mask in blocks:
                a = jnp.exp2(z_ref[slot, hh] + survives[slot, hh] + tail)
                if mask is not None:
                    a = jnp.where(mask, a, 0.0)
                weights[slot, hh] = a.astype(BF16)
                tail = tail - colsums[slot, hh]
            tail_ref[hh] = tail
        for slot, kb, _ in blocks:
            for hh in range(heads):
                acc_ref[hh] += jnp.dot(vt_ref[hh, kb], weights[slot, hh], preferred_element_type=F32)

    key = qi * tk + lax.broadcasted_iota(jnp.int32, (tk, tq), 0)
    qry = qi * tq + lax.broadcasted_iota(jnp.int32, (tk, tq), 1)
    before = jnp.maximum(qi - 1, 0)
    scores_to(0, qi)
    scores_to(1, before)
    visit([(0, qi, key < qry), (1, before, jnp.broadcast_to(qi > 0, (tk, tq)))])

    def any_live():
        return jnp.max(tail_ref[...]) > SB_DEAD_LOG2

    def body(c):
        kb = qi - 2 - c[0]
        scores_to(0, kb)
        visit([(0, kb, None)])
        return c[0] + 1, any_live()

    lax.while_loop(lambda c: jnp.logical_and(c[0] < qi - 1, c[1]), body, (jnp.int32(0), any_live()))
    for hh in range(heads):
        o_ref[:, _head_cols(hh)] = acc_ref[hh].T.astype(o_ref.dtype)


def _sb_attn(q, k, vt, *, batch, seq, scale):
    n, width = q.shape
    tq, tk, heads = SB_TQ, SB_TK, SB_HEADS_PER_STEP
    assert tq == tk
    q_spec, k_specs, vt_spec, o_spec = _attn_specs(seq, tq, tk, heads)
    return pl.pallas_call(
        functools.partial(_sb_attn_kernel, tq=tq, tk=tk, heads=heads, scale=scale),
        grid=(batch, width // (heads * HEAD_WIDTH), seq // tq),
        in_specs=[q_spec, *k_specs, vt_spec],
        out_specs=o_spec,
        out_shape=jax.ShapeDtypeStruct((n, width), BF16),
        scratch_shapes=[
            pltpu.VMEM((heads, HEAD_WIDTH, tq), BF16),
            pltpu.VMEM((tk, tk), BF16),
            pltpu.VMEM((2, heads, tk, tq), F32),
            pltpu.VMEM((heads, 1, tq), F32),
            pltpu.VMEM((heads, HEAD_WIDTH, tq), F32),
        ],
        compiler_params=_attn_params(seq, tq, tk, heads, tq),
        name="sb_attn",
    )(q, *[k] * heads, vt)


def kernel(x, positions, norm_mix_g, norm_ffn_g, norm_final_g, sc_w_in, sc_conv_w, sc_w_out, sg_w_in, sg_ln_g, sg_ln_b, sg_w_s, sg_b_s, sg_w_out, da_w_qkv, da_lambda_q1, da_lambda_k1, da_lambda_q2, da_lambda_k2, da_subln_g, da_w_out, sb_w_qkv, sb_w_out, ffn_w_gate, ffn_w_up, ffn_conv_w, ffn_conv_b, ffn_w_down):
    batch, seq, d = x.shape
    depth = norm_mix_g.shape[0]
    n = batch * seq
    assert seq % DIFF_TK == 0 and seq % DIFF_TQ == 0 and seq % SB_TQ == 0 and d % HEAD_WIDTH == 0

    def bf(w):
        return w.astype(BF16)

    w_down = bf(ffn_w_down)
    h = x.reshape(n, d)
    a = _rmsnorm(h, norm_mix_g[0])
    for layer in range(depth):
        mixer, j = layer % 4, layer // 4
        if mixer == 0:
            gb, u = _sc_in(a, bf(sc_w_in[j]))
            y = _sc_source(u, gb, sc_conv_w[j], seq=seq)
            w_out = sc_w_out
        elif mixer == 1:
            u, v = _sg_in(a, bf(sg_w_in[j]))
            y = _sg_source(u, v, sg_ln_g[j], sg_ln_b[j], sg_w_s[j], sg_b_s[j])
            w_out = sg_w_out
        elif mixer == 2:
            head_dim = d // DIFF_HEADS // 2
            lambda_init = 0.8 - 0.6 * math.exp(-0.3 * layer)
            tables = _rope_tables(positions, head_dim)
            q, k, vt = _qkv(a, bf(da_w_qkv[j]), tk=DIFF_TK, rope_tables=tables,
                            q_scale=head_dim ** -0.5 * LOG2E)
            y = _diff_attn(q, k, vt, da_lambda_q1[j], da_lambda_k1[j], da_lambda_q2[j], da_lambda_k2[j],
                           da_subln_g[j], batch=batch, seq=seq, lambda_init=lambda_init)
            w_out = da_w_out
        else:
            q, k, vt = _qkv(a, bf(sb_w_qkv[j]), tk=SB_TK)
            y = _sb_attn(q, k, vt, batch=batch, seq=seq, scale=(d // SB_HEADS) ** -0.5)
            w_out = sb_w_out
        h, a = _out_proj(y, bf(w_out), j, h, norm_ffn_g[layer])
        t = _ffn_up(a, ffn_w_gate, ffn_w_up, layer, ffn_conv_w[layer], ffn_conv_b[layer], seq=seq)
        last = layer == depth - 1
        h, a = _out_proj(t, w_down, layer, h, norm_final_g if last else norm_mix_g[layer + 1], final=last)
    return a.reshape(batch, seq, d)
```

```python
import functools
import math
from typing import Callable, NamedTuple, Sequence

import jax
import jax.numpy as jnp
from jax import lax
from jax.experimental import pallas as pl
from jax.experimental.pallas import tpu as pltpu

F32 = jnp.float32
BF16 = jnp.bfloat16

EPS = 1e-6
ROPE_THETA = 10000.0
CONV_WIDTH = 3
SGU_GROUPS = 16
SGU_CHUNK = 128
DIFF_HEADS = 16
SB_HEADS = 16
HEAD_WIDTH = 128

V7X_LANES = 128
V7X_SUBLANES = 8
V7X_VMEM_BYTES = 64 * 1024 * 1024
INTERNAL_SCRATCH_BYTES = 12 * 1024 * 1024


def _nbytes(shape, dtype):
    return math.prod(shape) * jnp.dtype(dtype).itemsize


def _params(semantics, pipelined_bytes, resident_bytes=0):
    limit = 2 * pipelined_bytes + resident_bytes + INTERNAL_SCRATCH_BYTES
    limit = min(limit, V7X_VMEM_BYTES - 4 * 1024 * 1024)
    return pltpu.CompilerParams(dimension_semantics=semantics, vmem_limit_bytes=int(limit))


def _row_tile(n, want):
    t = min(n, want)
    assert n % t == 0, (n, t)
    return t


def _rms(x, g):
    ms = jnp.mean(x * x, axis=-1, keepdims=True)
    return x * lax.rsqrt(ms + EPS) * g


def _shift_rows(x, prev, k):
    row = lax.broadcasted_iota(jnp.int32, x.shape, 0)
    out = pltpu.roll(x, k, 0)
    for r in range(k):
        out = jnp.where(row == r, prev[V7X_SUBLANES - k + r:V7X_SUBLANES - k + r + 1, :], out)
    return out


def _rmsnorm_kernel(x_ref, g_ref, o_ref):
    o_ref[...] = _rms(x_ref[...], g_ref[...]).astype(o_ref.dtype)


def _rmsnorm(h, g):
    n, d = h.shape
    tm = _row_tile(n, 512)
    return pl.pallas_call(
        _rmsnorm_kernel,
        grid=(n // tm,),
        in_specs=[pl.BlockSpec((tm, d), lambda i: (i, 0)), pl.BlockSpec((1, d), lambda i: (0, 0))],
        out_specs=pl.BlockSpec((tm, d), lambda i: (i, 0)),
        out_shape=jax.ShapeDtypeStruct((n, d), BF16),
        compiler_params=_params(("arbitrary",), _nbytes((tm, d), F32) + _nbytes((tm, d), BF16)),
        name="rmsnorm",
    )(h, g.reshape(1, d))


class _Source(NamedTuple):
    make_x: Callable
    inputs: Sequence
    specs: Callable
    row_bytes: int
    scratch: Callable
    resident_bytes: Callable
    name: str


def _plain_source(x):
    k = x.shape[1]
    return _Source(lambda tm: lambda x_ref: x_ref[...], [x],
                   lambda tm: [pl.BlockSpec((tm, k), lambda i: (i, 0))],
                   k * 2, lambda tm: [], lambda tm: 0, "out_proj")


def _out_proj_kernel(*refs, n_src, n_scratch, make_x, emit_h):
    src = refs[:n_src]
    w_ref, h_ref, g_ref = refs[n_src:n_src + 3]
    out_refs = refs[n_src + 3:len(refs) - n_scratch]
    scratch = refs[len(refs) - n_scratch:]
    hn = h_ref[...] + jnp.dot(make_x(*src, *scratch), w_ref[...], preferred_element_type=F32)
    if emit_h:
        out_refs[0][...] = hn
    an_ref = out_refs[-1]
    an_ref[...] = _rms(hn, g_ref[...]).astype(an_ref.dtype)


def _out_proj(source, w, layer, h, g, *, final=False):
    if not isinstance(source, _Source):
        source = _plain_source(source)
    n, d = h.shape
    k = w.shape[1]
    an_dtype = F32 if final else BF16

    def pipelined_bytes(tm):
        return tm * source.row_bytes + 2 * _nbytes((tm, d), F32) + _nbytes((tm, d), an_dtype)

    def resident_bytes(tm):
        return _nbytes((k, d), BF16) + source.resident_bytes(tm)

    budget = V7X_VMEM_BYTES - 4 * 1024 * 1024 - INTERNAL_SCRATCH_BYTES
    tm = next(t for t in (512, 256, 128)
              if n % t == 0 and 2 * pipelined_bytes(t) + resident_bytes(t) <= budget)
    out_shape = [jax.ShapeDtypeStruct((n, d), an_dtype)]
    out_specs = [pl.BlockSpec((tm, d), lambda i: (i, 0))]
    if not final:
        out_shape.insert(0, jax.ShapeDtypeStruct((n, d), F32))
        out_specs.insert(0, pl.BlockSpec((tm, d), lambda i: (i, 0)))
    scratch = source.scratch(tm)
    outs = pl.pallas_call(
        functools.partial(_out_proj_kernel, n_src=len(source.inputs), n_scratch=len(scratch),
                          make_x=source.make_x(tm), emit_h=not final),
        grid=(n // tm,),
        in_specs=[
            *source.specs(tm),
            pl.BlockSpec((None, k, d), lambda i: (layer, 0, 0), pipeline_mode=pl.Buffered(1)),
            pl.BlockSpec((tm, d), lambda i: (i, 0)),
            pl.BlockSpec((1, d), lambda i: (0, 0)),
        ],
        out_specs=out_specs,
        out_shape=out_shape,
        scratch_shapes=scratch,
        compiler_params=_params(("arbitrary",), pipelined_bytes(tm), resident_bytes(tm)),
        name=source.name,
    )(*source.inputs, w, h, g.reshape(1, d))
    return (None, outs[0]) if final else (outs[0], outs[1])


def _ffn_up_kernel(a_ref, wg_ref, wu_ref, cw_ref, cb_ref, t_ref, carry_ref, wg_bf_ref, wu_bf_ref, *,
                   tm, seq):
    i = pl.program_id(1)

    @pl.when(i == 0)
    def _():
        wg_bf_ref[...] = wg_ref[...].astype(BF16)
        wu_bf_ref[...] = wu_ref[...].astype(BF16)

    @pl.when((i * tm) % seq == 0)
    def _():
        carry_ref[...] = jnp.zeros_like(carry_ref)

    a = a_ref[...]
    g = jnp.dot(a, wg_bf_ref[...], preferred_element_type=F32)
    up = jnp.dot(a, wu_bf_ref[...], preferred_element_type=F32)
    prev = carry_ref[...]
    cw = cw_ref[...]
    conv = (cw[0:1, :] * _shift_rows(g, prev, 2) + cw[1:2, :] * _shift_rows(g, prev, 1)
            + cw[2:3, :] * g + cb_ref[...])
    t_ref[...] = (conv * jax.nn.sigmoid(conv) * up).astype(t_ref.dtype)
    carry_ref[...] = g[tm - V7X_SUBLANES:, :]


def _ffn_up(a, w_gate, w_up, layer, conv_w, conv_b, *, seq):
    n, d = a.shape
    f = w_gate.shape[2]
    tm = _row_tile(seq, 1024)
    tn = 512
    assert f % tn == 0
    pipelined = (_nbytes((tm, d), BF16) + 2 * _nbytes((d, tn), F32) + _nbytes((tm, tn), BF16)
                 + 2 * _nbytes((tm, tn), F32))
    w_spec = pl.BlockSpec((None, d, tn), lambda j, i: (layer, 0, j))
    return pl.pallas_call(
        functools.partial(_ffn_up_kernel, tm=tm, seq=seq),
        grid=(f // tn, n // tm),
        in_specs=[
            pl.BlockSpec((tm, d), lambda j, i: (i, 0)),
            w_spec,
            w_spec,
            pl.BlockSpec((CONV_WIDTH, tn), lambda j, i: (0, j)),
            pl.BlockSpec((1, tn), lambda j, i: (0, j)),
        ],
        out_specs=pl.BlockSpec((tm, tn), lambda j, i: (i, j)),
        out_shape=jax.ShapeDtypeStruct((n, f), BF16),
        scratch_shapes=[pltpu.VMEM((V7X_SUBLANES, tn), F32), pltpu.VMEM((d, tn), BF16),
                        pltpu.VMEM((d, tn), BF16)],
        compiler_params=_params(("arbitrary", "arbitrary"), pipelined, 2 * _nbytes((d, tn), BF16)),
        name="ffn_up",
    )(a, w_gate, w_up, conv_w, conv_b.reshape(1, f))


def _sc_in_kernel(a_ref, wb_ref, wc_ref, wx_ref, gb_ref, u_ref):
    a = a_ref[...]
    gb_ref[...] = jnp.dot(a, wb_ref[...], preferred_element_type=F32)
    gc = jnp.dot(a, wc_ref[...], preferred_element_type=F32)
    xi = jnp.dot(a, wx_ref[...], preferred_element_type=F32)
    u_ref[...] = gc * xi


def _sc_in(a, w_in):
    n, d = a.shape
    tm = _row_tile(n, 1024)
    tn = 512
    nj = d // tn
    pipelined = _nbytes((tm, d), BF16) + 3 * _nbytes((d, tn), BF16) + 3 * _nbytes((tm, tn), F32)
    return pl.pallas_call(
        _sc_in_kernel,
        grid=(nj, n // tm),
        in_specs=[
            pl.BlockSpec((tm, d), lambda j, i: (i, 0)),
            pl.BlockSpec((d, tn), lambda j, i: (0, j)),
            pl.BlockSpec((d, tn), lambda j, i: (0, j + nj)),
            pl.BlockSpec((d, tn), lambda j, i: (0, j + 2 * nj)),
        ],
        out_specs=[pl.BlockSpec((tm, tn), lambda j, i: (i, j))] * 2,
        out_shape=[jax.ShapeDtypeStruct((n, d), F32)] * 2,
        compiler_params=_params(("arbitrary", "arbitrary"), pipelined),
        name="sc_in",
    )(a, w_in, w_in, w_in)


def _sc_gate(u_ref, halo_ref, gb_ref, cw_ref, *, tm, seq):
    i = pl.program_id(0)
    u = u_ref[...]
    prev = jnp.where((i * tm) % seq == 0, 0.0, halo_ref[...])
    cw = cw_ref[...]
    conv = cw[0:1, :] * _shift_rows(u, prev, 2) + cw[1:2, :] * _shift_rows(u, prev, 1) + cw[2:3, :] * u
    return (gb_ref[...] * conv).astype(BF16)


def _sc_source(u, gb, conv_w, *, seq):
    d = u.shape[1]

    def specs(tm):
        assert seq % tm == 0
        hb = tm // V7X_SUBLANES
        return [
            pl.BlockSpec((tm, d), lambda i: (i, 0)),
            pl.BlockSpec((V7X_SUBLANES, d), lambda i: (jnp.maximum(i * hb - 1, 0), 0)),
            pl.BlockSpec((tm, d), lambda i: (i, 0)),
            pl.BlockSpec((CONV_WIDTH, d), lambda i: (0, 0)),
        ]

    return _Source(lambda tm: functools.partial(_sc_gate, tm=tm, seq=seq), [u, u, gb, conv_w], specs,
                   2 * d * 4, lambda tm: [], lambda tm: 2 * _nbytes((tm, d), F32), "sc_out")


def _sg_in_kernel(a_ref, wu_ref, wv_ref, u_ref, v_ref):
    a = a_ref[...]
    u_ref[...] = jax.nn.gelu(jnp.dot(a, wu_ref[...], preferred_element_type=F32))
    v_ref[...] = jax.nn.gelu(jnp.dot(a, wv_ref[...], preferred_element_type=F32))


def _sg_in(a, w_in):
    n, d = a.shape
    width = w_in.shape[1] // 2
    tm = _row_tile(n, 1024)
    tn = 512
    nj = width // tn
    pipelined = _nbytes((tm, d), BF16) + 2 * _nbytes((d, tn), BF16) + 2 * _nbytes((tm, tn), F32)
    return pl.pallas_call(
        _sg_in_kernel,
        grid=(nj, n // tm),
        in_specs=[
            pl.BlockSpec((tm, d), lambda j, i: (i, 0)),
            pl.BlockSpec((d, tn), lambda j, i: (0, j)),
            pl.BlockSpec((d, tn), lambda j, i: (0, j + nj)),
        ],
        out_specs=[pl.BlockSpec((tm, tn), lambda j, i: (i, j))] * 2,
        out_shape=[jax.ShapeDtypeStruct((n, width), F32)] * 2,
        compiler_params=_params(("arbitrary", "arbitrary"), pipelined),
        name="sg_in",
    )(a, w_in, w_in)


def _sg_gate(u_ref, v_ref, lng_ref, lnb_ref, ws_ref, bs_ref, vn_ref, y_ref, *, tm):
    v = v_ref[...]
    mu = jnp.mean(v, axis=-1, keepdims=True)
    vc = v - mu
    var = jnp.mean(vc * vc, axis=-1, keepdims=True)
    vn_ref[...] = (vc * lax.rsqrt(var + EPS) * lng_ref[...] + lnb_ref[...]).astype(vn_ref.dtype)
    t = SGU_CHUNK
    nchunk = tm // t
    row = lax.broadcasted_iota(jnp.int32, (t, t), 0)
    col = lax.broadcasted_iota(jnp.int32, (t, t), 1)
    bs = bs_ref[...]
    for g in range(SGU_GROUPS):
        cols = slice(g * t, (g + 1) * t)
        w = jnp.where(row >= col, ws_ref[g], 0.0).astype(BF16)
        rhs = jnp.concatenate([vn_ref[c * t:(c + 1) * t, cols] for c in range(nchunk)], axis=1)
        mixed = jnp.dot(w, rhs, preferred_element_type=F32) + bs[:, g:g + 1]
        for c in range(nchunk):
            rows = slice(c * t, (c + 1) * t)
            y_ref[rows, cols] = (u_ref[rows, cols] * mixed[:, c * t:(c + 1) * t]).astype(y_ref.dtype)
    return y_ref[...]


def _sg_source(u, v, ln_g, ln_b, w_s, b_s):
    width = u.shape[1]

    def specs(tm):
        assert tm % SGU_CHUNK == 0
        return [
            pl.BlockSpec((tm, width), lambda i: (i, 0)),
            pl.BlockSpec((tm, width), lambda i: (i, 0)),
            pl.BlockSpec((1, width), lambda i: (0, 0)),
            pl.BlockSpec((1, width), lambda i: (0, 0)),
            pl.BlockSpec(w_s.shape, lambda i: (0, 0, 0)),
            pl.BlockSpec((SGU_CHUNK, SGU_GROUPS), lambda i: (0, 0)),
        ]

    return _Source(lambda tm: functools.partial(_sg_gate, tm=tm),
                   [u, v, ln_g.reshape(1, width), ln_b.reshape(1, width), w_s, b_s.T], specs,
                   2 * width * 4, lambda tm: [pltpu.VMEM((tm, width), BF16)] * 2,
                   lambda tm: 2 * _nbytes(w_s.shape, F32) + 2 * _nbytes((tm, width), BF16), "sg_out")


DIFF_TQ = 512
DIFF_TK = 512
DIFF_HEADS_PER_STEP = 2
DIFF_CHUNK = 256
SB_TQ = 256
SB_TK = 256
SB_HEADS_PER_STEP = 4
SB_DEAD_LOG2 = -160.0
LOG2E = 1.4426950408889634


def _rope_table_kernel(pos_ref, invf_ref, sign_ref, cos_ref, sin_ref):
    ang = pos_ref[...] * invf_ref[...]
    cos_ref[...] = jnp.cos(ang)
    sin_ref[...] = jnp.sin(ang) * sign_ref[...]


def _rope_tables(positions, head_dim):
    n = positions.size
    half = head_dim // 2
    inv_freq = 1.0 / (ROPE_THETA ** (jnp.arange(0, head_dim, 2, dtype=F32) / head_dim))
    lane = jnp.arange(V7X_LANES)
    invf = inv_freq[lane % half].reshape(1, V7X_LANES)
    sign = jnp.where(lane % head_dim < half, -1.0, 1.0).astype(F32).reshape(1, V7X_LANES)
    tm = _row_tile(n, 1024)
    return pl.pallas_call(
        _rope_table_kernel,
        grid=(n // tm,),
        in_specs=[
            pl.BlockSpec((tm, 1), lambda i: (i, 0)),
            pl.BlockSpec((1, V7X_LANES), lambda i: (0, 0)),
            pl.BlockSpec((1, V7X_LANES), lambda i: (0, 0)),
        ],
        out_specs=[pl.BlockSpec((tm, V7X_LANES), lambda i: (i, 0))] * 2,
        out_shape=[jax.ShapeDtypeStruct((n, V7X_LANES), F32)] * 2,
        compiler_params=_params(("arbitrary",), 3 * _nbytes((tm, V7X_LANES), F32)),
        name="rope_tables",
    )(positions.astype(F32).reshape(n, 1), invf, sign)


def _rotate_half(x, cos, sin, first_half):
    partner = jnp.where(first_half, pltpu.roll(x, 96, 1), pltpu.roll(x, 32, 1))
    return x * cos + partner * sin


def _qkv_kernel(a_ref, wq_ref, wk_ref, wv_ref, *refs, rope, q_scale, tk):
    if rope:
        cos_ref, sin_ref, q_ref, k_ref, vt_ref = refs
    else:
        q_ref, k_ref, vt_ref = refs
    a = a_ref[...]
    q = jnp.dot(a, wq_ref[...], preferred_element_type=F32)
    k = jnp.dot(a, wk_ref[...], preferred_element_type=F32)
    v = jnp.dot(a, wv_ref[...], preferred_element_type=F32)
    tm, tn = q.shape
    if rope:
        cos = cos_ref[...]
        sin = sin_ref[...]
        lane = lax.broadcasted_iota(jnp.int32, cos.shape, 1)
        first_half = (lane % 64) < 32
        for c in range(tn // V7X_LANES):
            cols = slice(c * V7X_LANES, (c + 1) * V7X_LANES)
            q_ref[:, cols] = (_rotate_half(q[:, cols], cos, sin, first_half) * q_scale).astype(q_ref.dtype)
            k_ref[:, cols] = _rotate_half(k[:, cols], cos, sin, first_half).astype(k_ref.dtype)
    else:
        q_ref[...] = q.astype(q_ref.dtype)
        k_ref[...] = k.astype(k_ref.dtype)
    for hh in range(tn // HEAD_WIDTH):
        for kk in range(tm // tk):
            blk = v[kk * tk:(kk + 1) * tk, hh * HEAD_WIDTH:(hh + 1) * HEAD_WIDTH]
            vt_ref[hh, kk] = blk.T.astype(vt_ref.dtype)


def _qkv(a, w_qkv, *, tk, rope_tables=None, q_scale=1.0):
    n, d = a.shape
    width = w_qkv.shape[1] // 3
    tm = _row_tile(n, 1024)
    tn = 512
    nj = width // tn
    rope = rope_tables is not None
    in_specs = [
        pl.BlockSpec((tm, d), lambda j, i: (i, 0)),
        pl.BlockSpec((d, tn), lambda j, i: (0, j)),
        pl.BlockSpec((d, tn), lambda j, i: (0, j + nj)),
        pl.BlockSpec((d, tn), lambda j, i: (0, j + 2 * nj)),
    ]
    args = [a, w_qkv, w_qkv, w_qkv]
    if rope:
        in_specs += [pl.BlockSpec((tm, V7X_LANES), lambda j, i: (i, 0))] * 2
        args += list(rope_tables)
    heads = width // HEAD_WIDTH
    pipelined = (_nbytes((tm, d), BF16) + 3 * _nbytes((d, tn), BF16) + 3 * _nbytes((tm, tn), BF16)
                 + 3 * _nbytes((tm, tn), F32) + 2 * _nbytes((tm, V7X_LANES), F32))
    return pl.pallas_call(
        functools.partial(_qkv_kernel, rope=rope, q_scale=q_scale, tk=tk),
        grid=(nj, n // tm),
        in_specs=in_specs,
        out_specs=[
            pl.BlockSpec((tm, tn), lambda j, i: (i, j)),
            pl.BlockSpec((tm, tn), lambda j, i: (i, j)),
            pl.BlockSpec((tn // HEAD_WIDTH, tm // tk, HEAD_WIDTH, tk), lambda j, i: (j, i, 0, 0)),
        ],
        out_shape=[
            jax.ShapeDtypeStruct((n, width), BF16),
            jax.ShapeDtypeStruct((n, width), BF16),
            jax.ShapeDtypeStruct((heads, n // tk, HEAD_WIDTH, tk), BF16),
        ],
        compiler_params=_params(("arbitrary", "arbitrary"), pipelined),
        name="qkv_rope" if rope else "qkv",
    )(*args)


def _attn_specs(seq, tq, tk, heads):
    nq = seq // tq
    nk = seq // tk
    w = heads * HEAD_WIDTH
    q_spec = pl.BlockSpec((tq, w), lambda b, h, qi: (b * nq + qi, h))
    k_specs = [pl.BlockSpec((seq, HEAD_WIDTH), functools.partial(lambda b, h, qi, hh: (b, h * heads + hh), hh=hh))
               for hh in range(heads)]
    vt_spec = pl.BlockSpec((heads, nk, HEAD_WIDTH, tk), lambda b, h, qi: (h, b, 0, 0))
    o_spec = pl.BlockSpec((tq, w), lambda b, h, qi: (b * nq + qi, h))
    return q_spec, k_specs, vt_spec, o_spec


def _attn_params(seq, tq, tk, heads, score_cols):
    w = heads * HEAD_WIDTH
    pipelined = 2 * _nbytes((tq, w), BF16) + 2 * _nbytes((seq, w), BF16)
    scratch = (_nbytes((2, heads, tk, score_cols), F32) + _nbytes((2, heads, tk, score_cols), BF16)
               + _nbytes((heads, HEAD_WIDTH, score_cols), F32))
    return _params(("arbitrary", "arbitrary", "arbitrary"), pipelined, scratch)


def _head_cols(hh):
    return slice(hh * HEAD_WIDTH, (hh + 1) * HEAD_WIDTH)


def _diff_attn_kernel(q_ref, *refs, tq, tk, heads, lambda_init):
    k_refs = refs[:heads]
    (vt_ref, lq1_ref, lk1_ref, lq2_ref, lk2_ref, g_ref, o_ref,
     qq_ref, s_ref, p_ref, alpha_ref, m_ref, l_ref, acc_ref) = refs[heads:]
    qi = pl.program_id(2)
    feature = lax.broadcasted_iota(jnp.int32, (HEAD_WIDTH, tq), 0)
    for hh in range(heads):
        qt = q_ref[:, _head_cols(hh)].astype(F32).T
        qq_ref[hh, :, :tq] = jnp.where(feature < 64, qt, 0.0).astype(BF16)
        qq_ref[hh, :, tq:] = jnp.where(feature >= 64, qt, 0.0).astype(BF16)
        m_ref[hh] = jnp.full(m_ref.shape[1:], -jnp.inf, F32)
        l_ref[hh] = jnp.zeros(l_ref.shape[1:], F32)
        acc_ref[hh] = jnp.zeros(acc_ref.shape[1:], F32)

    chunks = [(hh, slice(c, c + DIFF_CHUNK)) for hh in range(heads) for c in range(0, 2 * tq, DIFF_CHUNK)]

    def live_keys(cols):
        return cols.start % tq + DIFF_CHUNK

    def scores_to(slot, kb, hh, cols, nkeys=tk):
        rows = pl.ds(pl.multiple_of(kb * tk, tk), nkeys)
        s_ref[slot, hh, cols.start // DIFF_CHUNK, :nkeys, :] = jnp.dot(
            k_refs[hh][rows, :], qq_ref[hh, :, cols], preferred_element_type=F32)

    def accumulate(slot, kb, hh, cols):
        ci = cols.start // DIFF_CHUNK
        acc_ref[hh, ci] = alpha_ref[slot, hh, :, cols] * acc_ref[hh, ci] + jnp.dot(
            vt_ref[hh, kb], p_ref[slot, hh, cols.start // DIFF_CHUNK], preferred_element_type=F32)

    def softmax_to(slot, kb, hh, cols, diagonal):
        ci = cols.start // DIFF_CHUNK
        nkeys = live_keys(cols) if diagonal else tk
        s = s_ref[slot, hh, ci, :nkeys, :]
        if diagonal:
            top = nkeys - DIFF_CHUNK
            tail = s[top:, :]
            key = lax.broadcasted_iota(jnp.int32, tail.shape, 0)
            qry = lax.broadcasted_iota(jnp.int32, tail.shape, 1)
            tail = jnp.where(key <= qry, tail, -jnp.inf)
            s = tail if top == 0 else jnp.concatenate([s[:top, :], tail], axis=0)
        m = m_ref[hh, :, cols]
        m_new = jnp.maximum(m, jnp.max(s, axis=0, keepdims=True))
        alpha = jnp.exp2(m - m_new)
        p = jnp.exp2(s - m_new)
        m_ref[hh, :, cols] = m_new
        l_ref[hh, :, cols] = alpha * l_ref[hh, :, cols] + jnp.sum(p, axis=0, keepdims=True)
        alpha_ref[slot, hh, :, cols] = alpha
        p_ref[slot, hh, ci, :nkeys, :] = p.astype(BF16)
        if nkeys < tk:
            p_ref[slot, hh, ci, nkeys:, :] = jnp.zeros((tk - nkeys, DIFF_CHUNK), BF16)

    def step(slot, kb, prev_kb, diagonal):
        other = 1 - slot
        for hh, cols in chunks:
            if not diagonal:
                accumulate(other, prev_kb, hh, cols)
            scores_to(other, jnp.maximum(kb - 1, 0), hh, cols)
            softmax_to(slot, kb, hh, cols, diagonal)

    nfull = (qi * tq) // tk
    for hh, cols in chunks:
        scores_to(0, nfull, hh, cols, live_keys(cols))
    step(0, nfull, None, True)

    def body(t, _):
        kb = nfull - 1 - t

        @pl.when(t % 2 == 0)
        def _():
            step(1, kb, kb + 1, False)

        @pl.when(t % 2 == 1)
        def _():
            step(0, kb, kb + 1, False)

        return 0

    lax.fori_loop(0, nfull, body, 0)

    @pl.when(nfull % 2 == 0)
    def _():
        for hh, cols in chunks:
            accumulate(0, 0, hh, cols)

    @pl.when(nfull % 2 == 1)
    def _():
        for hh, cols in chunks:
            accumulate(1, 0, hh, cols)

    lam = (jnp.exp(jnp.sum(lq1_ref[...] * lk1_ref[...], axis=-1, keepdims=True))
           - jnp.exp(jnp.sum(lq2_ref[...] * lk2_ref[...], axis=-1, keepdims=True)) + lambda_init)
    for hh in range(heads):
        for c in range(0, tq, DIFF_CHUNK):
            first, second = slice(c, c + DIFF_CHUNK), slice(tq + c, tq + c + DIFF_CHUNK)
            o = (acc_ref[hh, first.start // DIFF_CHUNK] / l_ref[hh, :, first]
                 - lam * (acc_ref[hh, second.start // DIFF_CHUNK] / l_ref[hh, :, second]))
            ms = jnp.mean(o * o, axis=0, keepdims=True)
            o = o * lax.rsqrt(ms + EPS) * g_ref[...] * (1.0 - lambda_init)
            o_ref[first, _head_cols(hh)] = o.T.astype(o_ref.dtype)


def _diff_attn(q, k, vt, lq1, lk1, lq2, lk2, subln_g, *, batch, seq, lambda_init):
    n, width = q.shape
    tq, tk, heads = DIFF_TQ, DIFF_TK, DIFF_HEADS_PER_STEP
    assert tq == tk
    q_spec, k_specs, vt_spec, o_spec = _attn_specs(seq, tq, tk, heads)
    dd = lq1.shape[0]
    small = pl.BlockSpec((1, dd), lambda b, h, qi: (0, 0))
    return pl.pallas_call(
        functools.partial(_diff_attn_kernel, tq=tq, tk=tk, heads=heads, lambda_init=lambda_init),
        grid=(batch, width // (heads * HEAD_WIDTH), seq // tq),
        in_specs=[q_spec, *k_specs, vt_spec, small, small, small, small,
                  pl.BlockSpec((HEAD_WIDTH, 1), lambda b, h, qi: (0, 0))],
        out_specs=o_spec,
        out_shape=jax.ShapeDtypeStruct((n, width), BF16),
        scratch_shapes=[
            pltpu.VMEM((heads, HEAD_WIDTH, 2 * tq), BF16),
            pltpu.VMEM((2, heads, 2 * tq // DIFF_CHUNK, tk, DIFF_CHUNK), F32),
            pltpu.VMEM((2, heads, 2 * tq // DIFF_CHUNK, tk, DIFF_CHUNK), BF16),
            pltpu.VMEM((2, heads, 1, 2 * tq), F32),
            pltpu.VMEM((heads, 1, 2 * tq), F32),
            pltpu.VMEM((heads, 1, 2 * tq), F32),
            pltpu.VMEM((heads, 2 * tq // DIFF_CHUNK, HEAD_WIDTH, DIFF_CHUNK), F32),
        ],
        compiler_params=_attn_params(seq, tq, tk, heads, 2 * tq),
        name="diff_attn",
    )(q, *[k] * heads, vt, lq1.reshape(1, dd), lk1.reshape(1, dd), lq2.reshape(1, dd), lk2.reshape(1, dd),
      subln_g.reshape(HEAD_WIDTH, 1))


def _sb_attn_kernel(q_ref, *refs, tq, tk, heads, scale):
    k_refs = refs[:heads]
    vt_ref, o_ref, qt_ref, later_ref, z_ref, tail_ref, acc_ref = refs[heads:]
    qi = pl.program_id(2)
    row = lax.broadcasted_iota(jnp.int32, (tk, tk), 0)
    col = lax.broadcasted_iota(jnp.int32, (tk, tk), 1)
    later_ref[...] = jnp.where(col > row, -1.0, 0.0).astype(BF16)
    for hh in range(heads):
        qt_ref[hh] = q_ref[:, _head_cols(hh)].astype(F32).T.astype(BF16)
    tail_ref[...] = jnp.zeros_like(tail_ref)
    acc_ref[...] = jnp.zeros_like(acc_ref)

    def scores_to(slot, kb):
        rows = pl.ds(pl.multiple_of(kb * tk, tk), tk)
        for hh in range(heads):
            z_ref[slot, hh] = jnp.dot(k_refs[hh][rows, :], qt_ref[hh],
                                      preferred_element_type=F32) * (scale * LOG2E)

    def visit(blocks):
        neg_later = later_ref[...]
        his, los, colsums = {}, {}, {}
        for slot, _, mask in blocks:
            for hh in range(heads):
                z = z_ref[slot, hh]
                softplus = jnp.maximum(z, 0.0) + jnp.log2(1.0 + jnp.exp2(-jnp.abs(z)))
                z_ref[slot, hh] = z - softplus
                if mask is not None:
                    softplus = jnp.where(mask, softplus, 0.0)
                hi = softplus.astype(BF16)
                his[slot, hh] = hi
                los[slot, hh] = (softplus - hi.astype(F32)).astype(BF16)
                colsums[slot, hh] = jnp.sum(softplus, axis=0, keepdims=True)
        survives = {key: jnp.dot(neg_later, his[key], preferred_element_type=F32)
                    + jnp.dot(neg_later, los[key], preferred_element_type=F32) for key in his}
        weights = {}
        for hh in range(heads):
            tail = tail_ref[hh]
            for slot, _, mask in blocks:
                a = jnp.exp2(z_ref[slot, hh] + survives[slot, hh] + tail)
                if mask is not None:
                    a = jnp.where(mask, a, 0.0)
                weights[slot, hh] = a.astype(BF16)
                tail = tail - colsums[slot, hh]
            tail_ref[hh] = tail
        for slot, kb, _ in blocks:
            for hh in range(heads):
                acc_ref[hh] += jnp.dot(vt_ref[hh, kb], weights[slot, hh], preferred_element_type=F32)

    key = qi * tk + lax.broadcasted_iota(jnp.int32, (tk, tq), 0)
    qry = qi * tq + lax.broadcasted_iota(jnp.int32, (tk, tq), 1)
    before = jnp.maximum(qi - 1, 0)
    scores_to(0, qi)
    scores_to(1, before)
    visit([(0, qi, key < qry), (1, before, jnp.broadcast_to(qi > 0, (tk, tq)))])

    def any_live():
        return jnp.max(tail_ref[...]) > SB_DEAD_LOG2

    def body(c):
        kb = qi - 2 - c[0]
        scores_to(0, kb)
        visit([(0, kb, None)])
        return c[0] + 1, any_live()

    lax.while_loop(lambda c: jnp.logical_and(c[0] < qi - 1, c[1]), body, (jnp.int32(0), any_live()))
    for hh in range(heads):
        o_ref[:, _head_cols(hh)] = acc_ref[hh].T.astype(o_ref.dtype)


def _sb_attn(q, k, vt, *, batch, seq, scale):
    n, width = q.shape
    tq, tk, heads = SB_TQ, SB_TK, SB_HEADS_PER_STEP
    assert tq == tk
    q_spec, k_specs, vt_spec, o_spec = _attn_specs(seq, tq, tk, heads)
    return pl.pallas_call(
        functools.partial(_sb_attn_kernel, tq=tq, tk=tk, heads=heads, scale=scale),
        grid=(batch, width // (heads * HEAD_WIDTH), seq // tq),
        in_specs=[q_spec, *k_specs, vt_spec],
        out_specs=o_spec,
        out_shape=jax.ShapeDtypeStruct((n, width), BF16),
        scratch_shapes=[
            pltpu.VMEM((heads, HEAD_WIDTH, tq), BF16),
            pltpu.VMEM((tk, tk), BF16),
            pltpu.VMEM((2, heads, tk, tq), F32),
            pltpu.VMEM((heads, 1, tq), F32),
            pltpu.VMEM((heads, HEAD_WIDTH, tq), F32),
        ],
        compiler_params=_attn_params(seq, tq, tk, heads, tq),
        name="sb_attn",
    )(q, *[k] * heads, vt)


def kernel(x, positions, norm_mix_g, norm_ffn_g, norm_final_g, sc_w_in, sc_conv_w, sc_w_out, sg_w_in, sg_ln_g, sg_ln_b, sg_w_s, sg_b_s, sg_w_out, da_w_qkv, da_lambda_q1, da_lambda_k1, da_lambda_q2, da_lambda_k2, da_subln_g, da_w_out, sb_w_qkv, sb_w_out, ffn_w_gate, ffn_w_up, ffn_conv_w, ffn_conv_b, ffn_w_down):
    batch, seq, d = x.shape
    depth = norm_mix_g.shape[0]
    n = batch * seq
    assert seq % DIFF_TK == 0 and seq % DIFF_TQ == 0 and seq % SB_TQ == 0 and d % HEAD_WIDTH == 0

    def bf(w):
        return w.astype(BF16)

    w_down = bf(ffn_w_down)
    h = x.reshape(n, d)
    a = _rmsnorm(h, norm_mix_g[0])
    for layer in range(depth):
        mixer, j = layer % 4, layer // 4
        if mixer == 0:
            gb, u = _sc_in(a, bf(sc_w_in[j]))
            y = _sc_source(u, gb, sc_conv_w[j], seq=seq)
            w_out = sc_w_out
        elif mixer == 1:
            u, v = _sg_in(a, bf(sg_w_in[j]))
            y = _sg_source(u, v, sg_ln_g[j], sg_ln_b[j], sg_w_s[j], sg_b_s[j])
            w_out = sg_w_out
        elif mixer == 2:
            head_dim = d // DIFF_HEADS // 2
            lambda_init = 0.8 - 0.6 * math.exp(-0.3 * layer)
            tables = _rope_tables(positions, head_dim)
            q, k, vt = _qkv(a, bf(da_w_qkv[j]), tk=DIFF_TK, rope_tables=tables,
                            q_scale=head_dim ** -0.5 * LOG2E)
            y = _diff_attn(q, k, vt, da_lambda_q1[j], da_lambda_k1[j], da_lambda_q2[j], da_lambda_k2[j],
                           da_subln_g[j], batch=batch, seq=seq, lambda_init=lambda_init)
            w_out = da_w_out
        else:
            q, k, vt = _qkv(a, bf(sb_w_qkv[j]), tk=SB_TK)
            y = _sb_attn(q, k, vt, batch=batch, seq=seq, scale=(d // SB_HEADS) ** -0.5)
            w_out = sb_w_out
        h, a = _out_proj(y, bf(w_out), j, h, norm_ffn_g[layer])
        t = _ffn_up(a, ffn_w_gate, ffn_w_up, layer, ffn_conv_w[layer], ffn_conv_b[layer], seq=seq)
        last = layer == depth - 1
        h, a = _out_proj(t, w_down, layer, h, norm_final_g if last else norm_mix_g[layer + 1], final=last)
    return a.reshape(batch, seq, d)
```
